```python
import jax, jax.numpy as jnp
from jax import lax
import numpy as np

D_MODEL = 1024
BATCH = 16
SEQ = 256
DEPTH = 2
DEC_BATCH = 4
DEC_SEQ = 1024
PAST_LEN = 512

GRID_W = 64
HEAD_DIM = 128
N_Q_HEADS = 8
N_KV_HEADS = 2
WINDOW = 128
BLOCK = 128
ROPE_BASE = 10000.0
DN_HEADS = 8
DN_DK = 128
DN_DV = 128
DN_CONV = 5
DN_CHUNK = 64
D_FF = 2816
N_EXPERTS = 8
TOP_K = 2
D_EXPERT = 3584
N_DENSE = (DEPTH + 1) // 2
N_MOE = DEPTH // 2
EPS = 1e-6
NEG_INF = -1e30

Q_W = N_Q_HEADS * HEAD_DIM
KV_W = N_KV_HEADS * HEAD_DIM
DN_KW = DN_HEADS * DN_DK
DN_VW = DN_HEADS * DN_DV
DN_QKV = 2 * DN_KW + DN_VW
SPLITS = (Q_W, KV_W, KV_W, DN_QKV, DN_VW, 2 * DN_HEADS, 2 * DN_HEADS, D_MODEL, D_MODEL)
IN_W = sum(SPLITS)

kernel_name = "hybrid_diffusion_swa_deltanet_prefix_ctx"

F32 = jnp.float32


def rmsnorm(x, g):
    xf = x.astype(F32)
    y = xf * lax.rsqrt(jnp.mean(xf * xf, axis=-1, keepdims=True) + EPS) * g.astype(F32)
    return y.astype(x.dtype)


def l2norm(x):
    return x * lax.rsqrt(jnp.sum(x * x, axis=-1, keepdims=True) + EPS)


def axial_rope_tables(n_rows):
    rows = jnp.repeat(jnp.arange(n_rows), GRID_W).astype(F32)
    cols = jnp.tile(jnp.arange(GRID_W), n_rows).astype(F32)
    n_freq = HEAD_DIM // 4
    inv = ROPE_BASE ** (-jnp.arange(n_freq, dtype=F32) / n_freq)
    ar = rows[:, None] * inv
    ac = cols[:, None] * inv
    return (jnp.cos(ar), jnp.sin(ar), jnp.cos(ac), jnp.sin(ac))


def _rotate(x, cos, sin):
    x1, x2 = jnp.split(x, 2, axis=-1)
    cos = cos[None, :, None, :]
    sin = sin[None, :, None, :]
    return jnp.concatenate([x1 * cos - x2 * sin, x2 * cos + x1 * sin], axis=-1)


def axial_rope(x, tables):
    cos_r, sin_r, cos_c, sin_c = tables
    xf = x.astype(F32)
    half = HEAD_DIM // 2
    y = jnp.concatenate([_rotate(xf[..., :half], cos_r, sin_r), _rotate(xf[..., half:], cos_c, sin_c)], axis=-1)
    return y.astype(x.dtype)


def softmax_with_sink(q, k, v, mask, sink):
    B, Q, Hq, d = q.shape
    Hkv = k.shape[2]
    G = Hq // Hkv
    qg = q.reshape(B, Q, Hkv, G, d)
    s = jnp.einsum('bqhgd,bkhd->bhgqk', qg, k).astype(F32) * (d ** -0.5)
    s = jnp.where(mask, s, NEG_INF)
    sk = jnp.broadcast_to(sink.astype(F32).reshape(1, Hkv, G, 1, 1), (B, Hkv, G, Q, 1))
    p = jax.nn.softmax(jnp.concatenate([s, sk], axis=-1), axis=-1)[..., :-1]
    o = jnp.einsum('bhgqk,bkhd->bqhgd', p.astype(v.dtype), v)
    return o.reshape(B, Q, Hq, d)


def context_attention(q, k, v, sink):
    B, L, Hq, d = q.shape
    nb = L // BLOCK
    qb = jnp.moveaxis(q.reshape(B, nb, BLOCK, Hq, d), 1, 0)
    mask = jnp.ones((BLOCK, L), bool)
    o = lax.map(lambda qi: softmax_with_sink(qi, k, v, mask, sink), qb)
    return jnp.moveaxis(o, 0, 1).reshape(B, L, Hq, d)


def latent_attention(q, k, v, k_ctx, v_ctx, sink):
    B, T, Hq, d = q.shape
    nb = T // BLOCK
    pad = ((0, 0), (BLOCK, BLOCK), (0, 0), (0, 0))
    kp = jnp.pad(k, pad)
    vp = jnp.pad(v, pad)
    k_ctx = k_ctx.astype(k.dtype)
    v_ctx = v_ctx.astype(v.dtype)
    ctx_mask = jnp.ones((BLOCK, k_ctx.shape[1]), bool)
    qb = jnp.moveaxis(q.reshape(B, nb, BLOCK, Hq, d), 1, 0)
    offs = jnp.arange(BLOCK)
    koffs = jnp.arange(3 * BLOCK)

    def one(args):
        i, qi = args
        start = i * BLOCK
        kl = lax.dynamic_slice_in_dim(kp, start, 3 * BLOCK, axis=1)
        vl = lax.dynamic_slice_in_dim(vp, start, 3 * BLOCK, axis=1)
        qpos = start + offs
        kpos = start - BLOCK + koffs
        band = (jnp.abs(qpos[:, None] - kpos[None, :]) <= WINDOW) & (kpos >= 0)[None, :] & (kpos < T)[None, :]
        keys = jnp.concatenate([k_ctx, kl], axis=1)
        vals = jnp.concatenate([v_ctx, vl], axis=1)
        mask = jnp.concatenate([ctx_mask, band], axis=1)
        return softmax_with_sink(qi, keys, vals, mask, sink)

    o = lax.map(one, (jnp.arange(nb), qb))
    return jnp.moveaxis(o, 0, 1).reshape(B, T, Hq, d)


def short_conv(x, w):
    C = x.shape[-1]
    return lax.conv_general_dilated(
        x, w[:, None, :].astype(x.dtype), window_strides=(1,),
        padding=((DN_CONV // 2, DN_CONV // 2),),
        dimension_numbers=('NWC', 'WIO', 'NWC'), feature_group_count=C)


def delta_inputs(qkv, beta_logit, a_in, conv_w, a_log, dt_bias):
    B, T, _ = qkv.shape
    qkv = jax.nn.silu(short_conv(qkv, conv_w)).astype(F32)
    q, k, v = jnp.split(qkv, [DN_KW, 2 * DN_KW], axis=-1)
    q = l2norm(q.reshape(B, T, DN_HEADS, DN_DK)) * (DN_DK ** -0.5)
    k = l2norm(k.reshape(B, T, DN_HEADS, DN_DK))
    v = v.reshape(B, T, DN_HEADS, DN_DV)
    beta = jax.nn.sigmoid(beta_logit.astype(F32)).reshape(B, T, 2, DN_HEADS)
    g = -jnp.exp(a_log.astype(F32)) * jax.nn.softplus(
        a_in.astype(F32).reshape(B, T, 2, DN_HEADS) + dt_bias.astype(F32))
    return q, k, v, g, beta


def gated_delta_chunked(q, k, v, g, beta, s0):
    B, T, H, _ = q.shape
    n = T // DN_CHUNK

    def chunks(x):
        x = x.reshape((B, n, DN_CHUNK, H) + x.shape[3:])
        return jnp.swapaxes(jnp.moveaxis(x, 1, 0), 2, 3)

    qc, kc, vc, bc = chunks(q), chunks(k), chunks(v), chunks(beta)
    gc = jnp.cumsum(chunks(g), axis=-1)
    idx = jnp.arange(DN_CHUNK)
    incl = idx[:, None] >= idx[None, :]
    strict = idx[:, None] > idx[None, :]
    diff = gc[..., :, None] - gc[..., None, :]
    decay = jnp.where(incl, jnp.exp(jnp.where(incl, diff, 0.0)), 0.0)
    kb = kc * bc[..., None]
    eye = jnp.eye(DN_CHUNK, dtype=F32)
    a_mat = jnp.where(strict, jnp.einsum('nbhik,nbhjk->nbhij', kb, kc) * decay, 0.0) + eye
    t_inv = lax.linalg.triangular_solve(a_mat, jnp.broadcast_to(eye, a_mat.shape),
                                        left_side=True, lower=True, unit_diagonal=True)
    u = jnp.einsum('nbhij,nbhjv->nbhiv', t_inv, vc * bc[..., None])
    w = jnp.einsum('nbhij,nbhjk->nbhik', t_inv, kb * jnp.exp(gc)[..., None])
    qk = jnp.einsum('nbhik,nbhjk->nbhij', qc, kc) * decay
    q_dec = qc * jnp.exp(gc)[..., None]
    g_last = gc[..., -1]
    k_dec = kc * jnp.exp(g_last[..., None] - gc)[..., None]

    def step(S, xs):
        u_c, w_c, qk_c, qd_c, kd_c, gl_c = xs
        v_new = u_c - jnp.einsum('bhck,bhkv->bhcv', w_c, S)
        o_c = jnp.einsum('bhck,bhkv->bhcv', qd_c, S) + jnp.einsum('bhij,bhjv->bhiv', qk_c, v_new)
        S = S * jnp.exp(gl_c)[..., None, None] + jnp.einsum('bhck,bhcv->bhkv', kd_c, v_new)
        return S, o_c

    s_fin, o = lax.scan(step, s0, (u, w, qk, q_dec, k_dec, g_last))
    o = jnp.moveaxis(jnp.swapaxes(o, 2, 3), 0, 1).reshape(B, T, H, DN_DV)
    return o, s_fin


def bidir_delta(q, k, v, g, beta, s0):
    flip = lambda t: jnp.flip(t, axis=1)
    o_f, s_f = gated_delta_chunked(q, k, v, g[:, :, 0], beta[:, :, 0], s0[:, 0])
    o_b, s_b = gated_delta_chunked(flip(q), flip(k), flip(v), flip(g[:, :, 1]), flip(beta[:, :, 1]), s0[:, 1])
    return o_f + flip(o_b), jnp.stack([s_f, s_b], axis=1)


def token_mixers(h, P, l, latent):
    B, T, _ = h.shape
    offsets = np.cumsum(SPLITS)[:-1].tolist()
    q_a, k_a, v_a, qkv_b, z_b, b_b, a_b, gt_a, gt_b = jnp.split(h @ P['w_in'][l], offsets, axis=-1)
    q_a = q_a.reshape(B, T, N_Q_HEADS, HEAD_DIM)
    k_a = k_a.reshape(B, T, N_KV_HEADS, HEAD_DIM)
    v_a = v_a.reshape(B, T, N_KV_HEADS, HEAD_DIM)
    sink = P['attn_sink'][l]
    q_b, k_b, v_b, g_b, beta_b = delta_inputs(qkv_b, b_b, a_b, P['conv_w'][l], P['dn_a_log'][l], P['dn_dt_bias'][l])
    if latent is None:
        o_a = context_attention(q_a, k_a, v_a, sink)
        s0 = jnp.zeros((B, 2, DN_HEADS, DN_DK, DN_DV), F32)
    else:
        rope, ctx_k, ctx_v, ctx_s = latent
        o_a = latent_attention(axial_rope(q_a, rope), axial_rope(k_a, rope), v_a, ctx_k, ctx_v, sink)
        s0 = ctx_s.astype(F32)
    o_b, s_fin = bidir_delta(q_b, k_b, v_b, g_b, beta_b, s0)
    o_b = rmsnorm(o_b, P['dn_norm_g'][l]) * jax.nn.silu(z_b.astype(F32).reshape(B, T, DN_HEADS, DN_DV))
    o_b = o_b.astype(h.dtype)
    br_a = o_a.reshape(B, T, Q_W) @ P['w_pa'][l]
    br_b = o_b.reshape(B, T, DN_VW) @ P['w_pb'][l]
    y = (jax.nn.sigmoid(gt_a) * br_a + jax.nn.sigmoid(gt_b) * br_b) @ P['w_o'][l]
    ctx_out = (k_a, v_a, s_fin.astype(h.dtype)) if latent is None else None
    return y, ctx_out


def swiglu(h, wg, wu, wd):
    return (jax.nn.silu(h @ wg) * (h @ wu)) @ wd


def moe_swiglu(h, w_router, wg, wu, wd):
    logits = (h @ w_router).astype(F32)
    top_v, top_i = lax.top_k(logits, TOP_K)
    wts = jax.nn.softmax(top_v, axis=-1)
    combine = jnp.sum(jax.nn.one_hot(top_i, N_EXPERTS, dtype=F32) * wts[..., None], axis=-2).astype(h.dtype)
    out = jnp.zeros_like(h)
    for e in range(N_EXPERTS):
        out = out + combine[..., e:e + 1] * swiglu(h, wg[e], wu[e], wd[e])
    return out


def sandwich_layer(x, cond, P, l, latent):
    mod = (jax.nn.silu(cond) @ P['w_ada'][l] + P['b_ada'][l])[:, None, :]
    sh1, sc1, gt1, sh2, sc2, gt2 = jnp.split(mod, 6, axis=-1)
    gn = P['norm_g'][l]
    h = rmsnorm(x, gn[0]) * (1 + sc1) + sh1
    y, ctx_out = token_mixers(h, P, l, latent)
    x = x + gt1 * rmsnorm(y, gn[1])
    h = rmsnorm(x, gn[2]) * (1 + sc2) + sh2
    if l % 2 == 0:
        i = l // 2
        f = swiglu(h, P['w_ff_gate'][i], P['w_ff_up'][i], P['w_ff_down'][i])
    else:
        i = l // 2
        f = moe_swiglu(h, P['w_router'][i], P['w_e_gate'][i], P['w_e_up'][i], P['w_e_down'][i])
    x = x + gt2 * rmsnorm(f, gn[3])
    return x, ctx_out


def setup_inputs(seed: int = 0) -> dict:
    key = jax.random.key(seed)
    keys = jax.random.split(key, 64)
    counter = [0]

    def nxt():
        counter[0] += 1
        return keys[counter[0] - 1]

    def rnd(shape, scale):
        return jax.random.normal(nxt(), shape, F32) * scale

    d = {}
    d['x_prompt'] = rnd((BATCH, SEQ, D_MODEL), 1.0)
    d['x_sample'] = rnd((DEC_BATCH, DEC_SEQ, D_MODEL), 1.0)
    d['c'] = rnd((DEC_BATCH, D_MODEL), 1.0)
    for l in range(DEPTH):
        d['cache_k_l%d' % l] = rnd((DEC_BATCH, PAST_LEN, N_KV_HEADS, HEAD_DIM), 1.0)
        d['cache_v_l%d' % l] = rnd((DEC_BATCH, PAST_LEN, N_KV_HEADS, HEAD_DIM), 1.0)
        d['state_delta_l%d' % l] = rnd((DEC_BATCH, 2, DN_HEADS, DN_DK, DN_DV), 0.5)
    d['c_ctx'] = rnd((D_MODEL,), 1.0)
    d['w_ada'] = rnd((DEPTH, D_MODEL, 6 * D_MODEL), 0.5 * D_MODEL ** -0.5)
    d['b_ada'] = rnd((DEPTH, 6 * D_MODEL), 0.02)
    d['norm_g'] = 1.0 + rnd((DEPTH, 4, D_MODEL), 0.02)
    d['w_in'] = rnd((DEPTH, D_MODEL, IN_W), D_MODEL ** -0.5)
    d['attn_sink'] = rnd((DEPTH, N_Q_HEADS), 0.5)
    d['conv_w'] = rnd((DEPTH, DN_CONV, DN_QKV), DN_CONV ** -0.5)
    d['dn_a_log'] = jnp.log(jax.random.uniform(nxt(), (DEPTH, 2, DN_HEADS), F32, 1.0, 16.0))
    dt = jnp.exp(jax.random.uniform(nxt(), (DEPTH, 2, DN_HEADS), F32, float(np.log(1e-3)), float(np.log(1e-1))))
    d['dn_dt_bias'] = dt + jnp.log(-jnp.expm1(-dt))
    d['dn_norm_g'] = 1.0 + rnd((DEPTH, DN_DV), 0.02)
    d['w_pa'] = rnd((DEPTH, Q_W, D_MODEL), Q_W ** -0.5)
    d['w_pb'] = rnd((DEPTH, DN_VW, D_MODEL), DN_VW ** -0.5)
    d['w_o'] = rnd((DEPTH, D_MODEL, D_MODEL), D_MODEL ** -0.5)
    d['w_ff_gate'] = rnd((N_DENSE, D_MODEL, D_FF), D_MODEL ** -0.5)
    d['w_ff_up'] = rnd((N_DENSE, D_MODEL, D_FF), D_MODEL ** -0.5)
    d['w_ff_down'] = rnd((N_DENSE, D_FF, D_MODEL), D_FF ** -0.5)
    d['w_router'] = rnd((N_MOE, D_MODEL, N_EXPERTS), D_MODEL ** -0.5)
    d['w_e_gate'] = rnd((N_MOE, N_EXPERTS, D_MODEL, D_EXPERT), D_MODEL ** -0.5)
    d['w_e_up'] = rnd((N_MOE, N_EXPERTS, D_MODEL, D_EXPERT), D_MODEL ** -0.5)
    d['w_e_down'] = rnd((N_MOE, N_EXPERTS, D_EXPERT, D_MODEL), D_EXPERT ** -0.5)
    return d


def reference(x_prompt, x_sample, c, cache_k_l0, cache_v_l0, state_delta_l0, cache_k_l1, cache_v_l1,
              state_delta_l1, c_ctx, w_ada, b_ada, norm_g, w_in, attn_sink, conv_w, dn_a_log, dn_dt_bias,
              dn_norm_g, w_pa, w_pb, w_o, w_ff_gate, w_ff_up, w_ff_down, w_router, w_e_gate, w_e_up, w_e_down):
    P = {'w_ada': w_ada, 'b_ada': b_ada, 'norm_g': norm_g, 'w_in': w_in, 'attn_sink': attn_sink,
         'conv_w': conv_w, 'dn_a_log': dn_a_log, 'dn_dt_bias': dn_dt_bias, 'dn_norm_g': dn_norm_g,
         'w_pa': w_pa, 'w_pb': w_pb, 'w_o': w_o, 'w_ff_gate': w_ff_gate, 'w_ff_up': w_ff_up,
         'w_ff_down': w_ff_down, 'w_router': w_router, 'w_e_gate': w_e_gate, 'w_e_up': w_e_up,
         'w_e_down': w_e_down}
    caches = ((cache_k_l0, cache_v_l0, state_delta_l0), (cache_k_l1, cache_v_l1, state_delta_l1))

    xp = x_prompt
    cond_ctx = c_ctx[None, :]
    ctx_state = []
    for l in range(DEPTH):
        xp, ctx_out = sandwich_layer(xp, cond_ctx, P, l, None)
        ctx_state.append(ctx_out)

    n_tok = x_sample.shape[1]
    ROWS = n_tok // GRID_W
    rope = axial_rope_tables(ROWS)
    xs = x_sample
    for l in range(DEPTH):
        k_c, v_c, s_c = caches[l]
        xs, _ = sandwich_layer(xs, c, P, l, (rope, k_c, v_c, s_c))

    return (xp, xs, ctx_state[0][0], ctx_state[0][1], ctx_state[0][2], ctx_state[1][0], ctx_state[1][1], ctx_state[1][2])
```

```python
import functools

import numpy as np
import jax
import jax.numpy as jnp
from jax import lax
from jax.experimental import pallas as pl
from jax.experimental.pallas import tpu as pltpu

F32 = jnp.float32
BF16 = jnp.bfloat16
HIGHEST = lax.Precision.HIGHEST

D_MODEL = 1024
HEAD_DIM = 128
N_Q_HEADS = 8
N_KV_HEADS = 2
GQA = N_Q_HEADS // N_KV_HEADS
WINDOW = 128
GRID_W = 64
ROPE_BASE = 10000.0
DN_HEADS = 8
DN_D = 128
DN_CONV = 5
DN_CHUNK = 64
N_EXPERTS = 8
EPS = 1e-6
NEG_INF = -1e30

LANES = 128
SUBLANES = 8

ROW_TILE = 256

C_Q = 0
C_GA = 1024
C_GB = 2048
C_Z = 3072
C_QB = 4096
C_KB = 5120
C_VB = 6144
C_K = 7168
C_V = 7424
C_BA = 7680
N_PROJ = 8192

VMEM_LIMIT = 56 * 1024 * 1024


def _params(*sem):
    return pltpu.CompilerParams(dimension_semantics=sem, vmem_limit_bytes=VMEM_LIMIT)


def _rms(x, g):
    return x * lax.rsqrt(jnp.mean(x * x, axis=-1, keepdims=True) + EPS) * g


def _silu(x):
    return x * jax.nn.sigmoid(x)


def _bdot(a, b):
    return jnp.dot(a.astype(BF16), b.astype(BF16), preferred_element_type=F32)


def _bdot_nt(a, b):
    return lax.dot_general(a.astype(BF16), b.astype(BF16), (((1,), (1,)), ((), ())),
                           preferred_element_type=F32)


def _fdot(a, b):
    return jnp.dot(a, b, precision=HIGHEST, preferred_element_type=F32)


def _ada_kernel(c_ref, w_ref, b_ref, o_ref):
    o_ref[0] = _fdot(_silu(c_ref[...]), w_ref[0]) + b_ref[0]


def ada_mod(cond, w_ada, b_ada):
    L = w_ada.shape[0]
    n = w_ada.shape[2] // D_MODEL
    return pl.pallas_call(
        _ada_kernel,
        grid=(L, n),
        in_specs=[pl.BlockSpec((SUBLANES, D_MODEL), lambda l, j: (0, 0)),
                  pl.BlockSpec((1, D_MODEL, D_MODEL), lambda l, j: (l, 0, j)),
                  pl.BlockSpec((1, 1, D_MODEL), lambda l, j: (l, 0, j))],
        out_specs=pl.BlockSpec((1, SUBLANES, D_MODEL), lambda l, j: (l, 0, j)),
        out_shape=jax.ShapeDtypeStruct((L, SUBLANES, 6 * D_MODEL), F32),
        compiler_params=_params("parallel", "parallel"),
        name="ada_mod",
    )(cond, w_ada, b_ada.reshape(L, 1, 6 * D_MODEL))


def _norm_mod_kernel(x_ref, g_ref, mod_ref, h_ref):
    h = _rms(x_ref[...], g_ref[...]) * (1.0 + mod_ref[0, 1:2, :]) + mod_ref[0, 0:1, :]
    h_ref[...] = h.astype(h_ref.dtype)


def norm_mod(x, g, modt):
    n = x.shape[0] // ROW_TILE
    return pl.pallas_call(
        _norm_mod_kernel,
        grid=(n,),
        in_specs=[pl.BlockSpec((ROW_TILE, D_MODEL), lambda i: (i, 0)),
                  pl.BlockSpec((1, D_MODEL), lambda i: (0, 0)),
                  pl.BlockSpec((1, 6, D_MODEL), lambda i: (i, 0, 0))],
        out_specs=pl.BlockSpec((ROW_TILE, D_MODEL), lambda i: (i, 0)),
        out_shape=jax.ShapeDtypeStruct(x.shape, BF16),
        compiler_params=_params("parallel"),
        name="norm_mod",
    )(x, g, modt)


def _mm_kernel(a_ref, b_ref, o_ref):
    o_ref[...] = jnp.dot(a_ref[...], b_ref[...], preferred_element_type=F32).astype(o_ref.dtype)


def matmul(a, b, tm, tn, out_dtype=F32):
    M, K = a.shape
    N = b.shape[1]
    return pl.pallas_call(
        _mm_kernel,
        grid=(N // tn, M // tm),
        in_specs=[pl.BlockSpec((tm, K), lambda j, i: (i, 0)),
                  pl.BlockSpec((K, tn), lambda j, i: (0, j))],
        out_specs=pl.BlockSpec((tm, tn), lambda j, i: (i, j)),
        out_shape=jax.ShapeDtypeStruct((M, N), out_dtype),
        compiler_params=_params("parallel", "parallel"),
        name="in_proj",
    )(a, b)


def _ctx_attn_kernel(sink_ref, q_ref, k_ref, v_ref, o_ref):
    hk = pl.program_id(1)
    k = k_ref[...].astype(BF16)
    v = v_ref[...].astype(BF16)
    scale = HEAD_DIM ** -0.5
    for g in range(GQA):
        q = q_ref[:, g * HEAD_DIM:(g + 1) * HEAD_DIM]
        s = _bdot_nt(q, k) * scale
        sink = sink_ref[hk * GQA + g]
        m = jnp.maximum(jnp.max(s, axis=-1, keepdims=True), sink)
        p = jnp.exp(s - m)
        den = jnp.sum(p, axis=-1, keepdims=True) + jnp.exp(sink - m)
        o = _bdot(p, v) / den
        o_ref[:, g * HEAD_DIM:(g + 1) * HEAD_DIM] = o.astype(o_ref.dtype)


def ctx_attention(proj, sink, n_seq, T):
    qw = GQA * HEAD_DIM
    return pl.pallas_call(
        _ctx_attn_kernel,
        grid_spec=pltpu.PrefetchScalarGridSpec(
            num_scalar_prefetch=1,
            grid=(n_seq, N_KV_HEADS),
            in_specs=[pl.BlockSpec((T, qw), lambda b, h, s: (b, C_Q // qw + h)),
                      pl.BlockSpec((T, HEAD_DIM), lambda b, h, s: (b, C_K // HEAD_DIM + h)),
                      pl.BlockSpec((T, HEAD_DIM), lambda b, h, s: (b, C_V // HEAD_DIM + h))],
            out_specs=pl.BlockSpec((T, qw), lambda b, h, s: (b, h))),
        out_shape=jax.ShapeDtypeStruct((n_seq * T, N_Q_HEADS * HEAD_DIM), BF16),
        compiler_params=_params("parallel", "parallel"),
        name="ctx_attn",
    )(sink, proj, proj, proj)


def _rope(x, cos, sin):
    lane = lax.broadcasted_iota(jnp.int32, x.shape, 1)
    q = HEAD_DIM // 4
    swapped = jnp.where((lane % (2 * q)) < q,
                        pltpu.roll(x, HEAD_DIM - q, 1), pltpu.roll(x, q, 1))
    return x * cos + swapped * sin


def _lat_attn_kernel(T, sink_ref, q_ref, k_ref, v_ref, kc_ref, vc_ref, cosq_ref, sinq_ref,
                     cos_ref, sin_ref, o_ref, kpad, vpad):
    hk = pl.program_id(1)
    i = pl.program_id(2)
    nb = pl.num_programs(2)
    B = WINDOW

    @pl.when(i == 0)
    def _():
        zeros = jnp.zeros((B, HEAD_DIM), BF16)
        kpad[0:B, :] = zeros
        kpad[T + B:T + 2 * B, :] = zeros
        vpad[0:B, :] = zeros
        vpad[T + B:T + 2 * B, :] = zeros
        kpad[B:T + B, :] = _rope(k_ref[...], cos_ref[...], sin_ref[...]).astype(BF16)
        vpad[B:T + B, :] = v_ref[...].astype(BF16)

    start = pl.multiple_of(i * B, B)
    kw = kpad[pl.ds(start, 3 * B), :]
    vw = vpad[pl.ds(start, 3 * B), :]
    kc = kc_ref[0].astype(BF16)
    vc = vc_ref[0].astype(BF16)

    rows = GQA * B
    qi = lax.broadcasted_iota(jnp.int32, (rows, 3 * B), 0) % B
    kj = lax.broadcasted_iota(jnp.int32, (rows, 3 * B), 1) - B
    kpos = kj + i * B
    mask = (jnp.abs(qi - kj) <= WINDOW) & (kpos >= 0) & (kpos < T)

    cq = cosq_ref[...]
    sq = sinq_ref[...]
    q = jnp.concatenate(
        [_rope(q_ref[:, g * HEAD_DIM:(g + 1) * HEAD_DIM], cq, sq) for g in range(GQA)], axis=0)
    scale = HEAD_DIM ** -0.5
    s_c = _bdot_nt(q, kc) * scale
    s_w = jnp.where(mask, _bdot_nt(q, kw) * scale, NEG_INF)
    gi = lax.broadcasted_iota(jnp.int32, (rows, 1), 0) // B
    sink = jnp.zeros((rows, 1), F32)
    for g in range(GQA):
        sink = jnp.where(gi == g, sink_ref[hk * GQA + g], sink)
    m = jnp.maximum(jnp.maximum(jnp.max(s_c, axis=-1, keepdims=True),
                                jnp.max(s_w, axis=-1, keepdims=True)), sink)
    p_c = jnp.exp(s_c - m)
    p_w = jnp.exp(s_w - m)
    den = (jnp.sum(p_c, axis=-1, keepdims=True) + jnp.sum(p_w, axis=-1, keepdims=True)
           + jnp.exp(sink - m))
    o = (_bdot(p_c, vc) + _bdot(p_w, vw)) / den
    for g in range(GQA):
        o_ref[:, g * HEAD_DIM:(g + 1) * HEAD_DIM] = o[g * B:(g + 1) * B, :].astype(o_ref.dtype)


def lat_attention(proj, sink, cache_k, cache_v, cos, sin, row0, n_seq, T):
    qw = GQA * HEAD_DIM
    B = WINDOW
    nb = T // B
    P = cache_k.shape[1]
    kc = cache_k.reshape(n_seq, P, N_KV_HEADS * HEAD_DIM)
    vc = cache_v.reshape(n_seq, P, N_KV_HEADS * HEAD_DIM)
    rb = row0 // B
    rt = row0 // T
    return pl.pallas_call(
        functools.partial(_lat_attn_kernel, T),
        grid_spec=pltpu.PrefetchScalarGridSpec(
            num_scalar_prefetch=1,
            grid=(n_seq, N_KV_HEADS, nb),
            in_specs=[pl.BlockSpec((B, qw), lambda b, h, i, s: (rb + b * nb + i, C_Q // qw + h)),
                      pl.BlockSpec((T, HEAD_DIM), lambda b, h, i, s: (rt + b, C_K // HEAD_DIM + h)),
                      pl.BlockSpec((T, HEAD_DIM), lambda b, h, i, s: (rt + b, C_V // HEAD_DIM + h)),
                      pl.BlockSpec((1, P, HEAD_DIM), lambda b, h, i, s: (b, 0, h)),
                      pl.BlockSpec((1, P, HEAD_DIM), lambda b, h, i, s: (b, 0, h)),
                      pl.BlockSpec((B, HEAD_DIM), lambda b, h, i, s: (i, 0)),
                      pl.BlockSpec((B, HEAD_DIM), lambda b, h, i, s: (i, 0)),
                      pl.BlockSpec((T, HEAD_DIM), lambda b, h, i, s: (0, 0)),
                      pl.BlockSpec((T, HEAD_DIM), lambda b, h, i, s: (0, 0))],
            out_specs=pl.BlockSpec((B, qw), lambda b, h, i, s: (b * nb + i, h)),
            scratch_shapes=[pltpu.VMEM((T + 2 * B, HEAD_DIM), BF16),
                            pltpu.VMEM((T + 2 * B, HEAD_DIM), BF16)]),
        out_shape=jax.ShapeDtypeStruct((n_seq * T, N_Q_HEADS * HEAD_DIM), BF16),
        compiler_params=_params("parallel", "parallel", "arbitrary"),
        name="lat_attn",
    )(sink, proj, proj, proj, kc, vc, cos, sin, cos, sin)


def rope_tables(T):
    rows = (np.arange(T) // GRID_W).astype(np.float32)
    cols = (np.arange(T) % GRID_W).astype(np.float32)
    n_freq = HEAD_DIM // 4
    inv = jnp.asarray(ROPE_BASE, F32) ** (-jnp.arange(n_freq, dtype=F32) / n_freq)
    ar = jnp.asarray(rows)[:, None] * inv
    ac = jnp.asarray(cols)[:, None] * inv
    cos = jnp.concatenate([jnp.cos(ar), jnp.cos(ar), jnp.cos(ac), jnp.cos(ac)], axis=-1)
    sin = jnp.concatenate([-jnp.sin(ar), jnp.sin(ar), -jnp.sin(ac), jnp.sin(ac)], axis=-1)
    return cos, sin


def _dn_kernel(T, has_s0, *refs):
    if has_s0:
        (q_ref, k_ref, v_ref, z_ref, ba_ref, wq_ref, wk_ref, wv_ref, alog_ref, dt_ref, gn_ref,
         s0_ref, o_ref, xpad, qs, ks, vs, cols, o_f, o_b, S) = refs
    else:
        (q_ref, k_ref, v_ref, z_ref, ba_ref, wq_ref, wk_ref, wv_ref, alog_ref, dt_ref, gn_ref,
         o_ref, sfin_ref, xpad, qs, ks, vs, cols, o_f, o_b, S) = refs
    h = pl.program_id(1)
    C = DN_CHUNK
    n = T // C
    half = DN_CONV // 2
    PAD = SUBLANES

    zpad = jnp.zeros((PAD, DN_D), F32)
    xpad[0:PAD, :] = zpad
    xpad[T + PAD:T + 2 * PAD, :] = zpad

    def conv_silu(x_ref, w_ref):
        xpad[PAD:T + PAD, :] = x_ref[...]
        w = w_ref[...]
        acc = xpad[pl.ds(PAD - half, T), :] * w[0:1, :]
        for j in range(1, DN_CONV):
            acc = acc + xpad[pl.ds(PAD - half + j, T), :] * w[j:j + 1, :]
        return _silu(acc)

    def l2n(x):
        return x * lax.rsqrt(jnp.sum(x * x, axis=-1, keepdims=True) + EPS)

    qs[...] = l2n(conv_silu(q_ref, wq_ref)) * (DN_D ** -0.5)
    ks[...] = l2n(conv_silu(k_ref, wk_ref))
    vs[...] = conv_silu(v_ref, wv_ref)

    li = lax.broadcasted_iota(jnp.int32, (LANES, LANES), 0)
    lj = lax.broadcasted_iota(jnp.int32, (LANES, LANES), 1)
    sel = jnp.where((lj < 4) & (li == lj * DN_HEADS + h), 1.0, 0.0).astype(F32)
    raw = _fdot(ba_ref[...], sel)
    lane = lax.broadcasted_iota(jnp.int32, (T, LANES), 1)
    gate = -jnp.exp(alog_ref[0]) * jax.nn.softplus(raw + dt_ref[0])
    cols[...] = jnp.where(lane < 2, jax.nn.sigmoid(raw), gate)

    if has_s0:
        S[0] = s0_ref[0, 0, 0]
        S[1] = s0_ref[0, 1, 0]
    else:
        S[...] = jnp.zeros((2, DN_D, DN_D), F32)

    ri = lax.broadcasted_iota(jnp.int32, (C, C), 0)
    ci = lax.broadcasted_iota(jnp.int32, (C, C), 1)
    eye = jnp.where(ri == ci, 1.0, 0.0).astype(F32)

    def chunk_step(d, cc):
        incl = (ri >= ci) if d == 0 else (ri <= ci)
        strict = (ri > ci) if d == 0 else (ri < ci)
        r0 = pl.multiple_of(cc * C, C)
        q = qs[pl.ds(r0, C), :]
        k = ks[pl.ds(r0, C), :]
        v = vs[pl.ds(r0, C), :]
        cb = cols[pl.ds(r0, C), :]
        gcum = _fdot(jnp.where(incl, 1.0, 0.0).astype(F32), cb)
        gcc = gcum[:, 2 + d:3 + d]
        bcol = cb[:, d:d + 1]
        tr = jnp.concatenate([k, gcum], axis=0).T
        kT = tr[:, 0:C]
        gcr = tr[2 + d:3 + d, C:2 * C]
        g_last = gcr[:, C - 1:C] if d == 0 else gcr[:, 0:1]
        diff = gcc - gcr
        decay = jnp.where(incl, jnp.exp(jnp.where(incl, diff, 0.0)), 0.0)
        kk = _bdot(k, kT)
        a_neg = jnp.where(strict, -(kk * bcol * decay), 0.0)
        t_inv = eye + a_neg
        pw = a_neg
        for _ in range(5):
            pw = _fdot(pw, pw)
            t_inv = t_inv + _fdot(t_inv, pw)
        egc = jnp.exp(gcc)
        rhs = jnp.concatenate([v * bcol, k * (bcol * egc)], axis=1)
        uw = _bdot(t_inv, rhs)
        u = uw[:, 0:DN_D]
        w = uw[:, DN_D:2 * DN_D]
        s_old = S[d]
        xs = _bdot(jnp.concatenate([w, q * egc], axis=0), s_old)
        v_new = u - xs[0:C, :]
        qk = _bdot(q, kT) * decay
        o = xs[C:2 * C, :] + _bdot(qk, v_new)
        kdT = kT * jnp.exp(g_last - gcr)
        S[d] = s_old * jnp.exp(g_last) + _bdot(kdT, v_new)
        if d == 0:
            o_f[pl.ds(r0, C), :] = o
        else:
            o_b[pl.ds(r0, C), :] = o

    def body(c, carry):
        chunk_step(0, c)
        chunk_step(1, n - 1 - c)
        return carry

    lax.fori_loop(0, n, body, 0)

    if not has_s0:
        sfin_ref[0, 0, 0] = S[0]
        sfin_ref[0, 1, 0] = S[1]

    y = _rms(o_f[...] + o_b[...], gn_ref[...]) * _silu(z_ref[...])
    o_ref[...] = y.astype(o_ref.dtype)


def deltanet(proj, conv_w, alog_l, dt_l, gn, s0, row0, n_seq, T):
    rt = row0 // T
    has_s0 = s0 is not None

    def col(c0):
        return pl.BlockSpec((T, DN_D), lambda b, h: (rt + b, c0 // DN_D + h))

    def wcol(c0):
        return pl.BlockSpec((DN_CONV, DN_D), lambda b, h: (0, c0 // DN_D + h))

    in_specs = [col(C_QB), col(C_KB), col(C_VB), col(C_Z),
                pl.BlockSpec((T, LANES), lambda b, h: (rt + b, C_BA // LANES)),
                wcol(0), wcol(DN_HEADS * DN_D), wcol(2 * DN_HEADS * DN_D),
                pl.BlockSpec((1, 1, LANES), lambda b, h: (h, 0, 0)),
                pl.BlockSpec((1, 1, LANES), lambda b, h: (h, 0, 0)),
                pl.BlockSpec((1, DN_D), lambda b, h: (0, 0))]
    args = [proj, proj, proj, proj, proj, conv_w, conv_w, conv_w, alog_l, dt_l, gn]
    o_spec = pl.BlockSpec((T, DN_D), lambda b, h: (b, h))
    o_shape = jax.ShapeDtypeStruct((n_seq * T, DN_HEADS * DN_D), BF16)
    s_spec = pl.BlockSpec((1, 2, 1, DN_D, DN_D), lambda b, h: (b, 0, h, 0, 0))
    if has_s0:
        in_specs.append(s_spec)
        args.append(s0)
        out_specs, out_shape = o_spec, o_shape
    else:
        out_specs = (o_spec, s_spec)
        out_shape = (o_shape, jax.ShapeDtypeStruct((n_seq, 2, DN_HEADS, DN_D, DN_D), F32))
    scratch = [pltpu.VMEM((T + 2 * SUBLANES, DN_D), F32)] + [pltpu.VMEM((T, DN_D), F32)] * 6 + [
        pltpu.VMEM((2, DN_D, DN_D), F32)]
    return pl.pallas_call(
        functools.partial(_dn_kernel, T, has_s0),
        grid=(n_seq, DN_HEADS),
        in_specs=in_specs,
        out_specs=out_specs,
        out_shape=out_shape,
        scratch_shapes=scratch,
        compiler_params=_params("parallel", "parallel"),
        name="deltanet_lat" if has_s0 else "deltanet_ctx",
    )(*args)


def _mix_kernel(route, oa_ref, ob_ref, ga_ref, gb_ref, x_ref, mod_ref, gn_ref, wpa_ref, wpb_ref,
                wo_ref, *rest):
    if route:
        wr_ref, x1_ref, h2_ref, lg_ref = rest
    else:
        x1_ref, h2_ref = rest
    br_a = jnp.dot(oa_ref[...], wpa_ref[...], preferred_element_type=F32)
    br_b = jnp.dot(ob_ref[...], wpb_ref[...], preferred_element_type=F32)
    merged = jax.nn.sigmoid(ga_ref[...]) * br_a + jax.nn.sigmoid(gb_ref[...]) * br_b
    y = jnp.dot(merged.astype(BF16), wo_ref[...], preferred_element_type=F32)
    x1 = x_ref[...] + mod_ref[0, 2:3, :] * _rms(y, gn_ref[1:2, :])
    x1_ref[...] = x1
    h2 = _rms(x1, gn_ref[2:3, :]) * (1.0 + mod_ref[0, 4:5, :]) + mod_ref[0, 3:4, :]
    h2_ref[...] = h2.astype(h2_ref.dtype)
    if route:
        lg_ref[...] = _fdot(h2, wr_ref[...])


def mix(o_a, o_b, proj, x, modt, gn, wpa, wpb, wo, w_router):
    n_tok = x.shape[0]
    n = n_tok // ROW_TILE
    route = w_router is not None
    row = lambda i: (i, 0)
    full = lambda i: (0, 0)
    in_specs = [pl.BlockSpec((ROW_TILE, D_MODEL), row),
                pl.BlockSpec((ROW_TILE, D_MODEL), row),
                pl.BlockSpec((ROW_TILE, D_MODEL), lambda i: (i, C_GA // D_MODEL)),
                pl.BlockSpec((ROW_TILE, D_MODEL), lambda i: (i, C_GB // D_MODEL)),
                pl.BlockSpec((ROW_TILE, D_MODEL), row),
                pl.BlockSpec((1, 6, D_MODEL), lambda i: (i, 0, 0)),
                pl.BlockSpec((4, D_MODEL), full),
                pl.BlockSpec((D_MODEL, D_MODEL), full),
                pl.BlockSpec((D_MODEL, D_MODEL), full),
                pl.BlockSpec((D_MODEL, D_MODEL), full)]
    args = [o_a, o_b, proj, proj, x, modt, gn, wpa, wpb, wo]
    out_specs = [pl.BlockSpec((ROW_TILE, D_MODEL), row), pl.BlockSpec((ROW_TILE, D_MODEL), row)]
    out_shape = [jax.ShapeDtypeStruct((n_tok, D_MODEL), F32),
                 jax.ShapeDtypeStruct((n_tok, D_MODEL), BF16)]
    if route:
        in_specs.append(pl.BlockSpec((D_MODEL, LANES), full))
        args.append(w_router)
        out_specs.append(pl.BlockSpec((ROW_TILE, LANES), row))
        out_shape.append(jax.ShapeDtypeStruct((n_tok, LANES), F32))
    return pl.pallas_call(
        functools.partial(_mix_kernel, route),
        grid=(n,),
        in_specs=in_specs,
        out_specs=tuple(out_specs),
        out_shape=tuple(out_shape),
        compiler_params=_params("parallel"),
        name="mix_route" if route else "mix",
    )(*args)


FFN_TM = 512


def _ffn_kernel(next_norm, h_ref, wg_ref, wu_ref, wd_ref, x_ref, mod_ref, gn_ref, *rest):
    if next_norm:
        modn_ref, gnn_ref, x2_ref, hn_ref, acc = rest
    else:
        x2_ref, acc = rest
    f = pl.program_id(1)
    h = h_ref[...]
    gate = jnp.dot(h, wg_ref[...], preferred_element_type=F32)
    up = jnp.dot(h, wu_ref[...], preferred_element_type=F32)
    part = jnp.dot((_silu(gate) * up).astype(BF16), wd_ref[...], preferred_element_type=F32)

    @pl.when(f == 0)
    def _():
        acc[...] = part

    @pl.when(f > 0)
    def _():
        acc[...] += part

    @pl.when(f == pl.num_programs(1) - 1)
    def _():
        for r in range(FFN_TM // ROW_TILE):
            rs = slice(r * ROW_TILE, (r + 1) * ROW_TILE)
            x2 = x_ref[rs, :] + mod_ref[r, 5:6, :] * _rms(acc[rs, :], gn_ref[3:4, :])
            x2_ref[rs, :] = x2
            if next_norm:
                hn = _rms(x2, gnn_ref[0:1, :]) * (1.0 + modn_ref[r, 1:2, :]) + modn_ref[r, 0:1, :]
                hn_ref[rs, :] = hn.astype(hn_ref.dtype)


def ffn_dense(h2, wg, wu, wd, x1, modt, gn, modt_next, gn_next, tf):
    n_tok = x1.shape[0]
    F = wg.shape[1]
    nr = FFN_TM // ROW_TILE
    next_norm = modt_next is not None
    row = lambda i, f: (i, 0)
    in_specs = [pl.BlockSpec((FFN_TM, D_MODEL), row),
                pl.BlockSpec((D_MODEL, tf), lambda i, f: (0, f)),
                pl.BlockSpec((D_MODEL, tf), lambda i, f: (0, f)),
                pl.BlockSpec((tf, D_MODEL), lambda i, f: (f, 0)),
                pl.BlockSpec((FFN_TM, D_MODEL), row),
                pl.BlockSpec((nr, 6, D_MODEL), lambda i, f: (i, 0, 0)),
                pl.BlockSpec((4, D_MODEL), lambda i, f: (0, 0))]
    args = [h2, wg, wu, wd, x1, modt, gn]
    out_specs = [pl.BlockSpec((FFN_TM, D_MODEL), row)]
    out_shape = [jax.ShapeDtypeStruct((n_tok, D_MODEL), F32)]
    if next_norm:
        in_specs += [pl.BlockSpec((nr, 6, D_MODEL), lambda i, f: (i, 0, 0)),
                     pl.BlockSpec((4, D_MODEL), lambda i, f: (0, 0))]
        args += [modt_next, gn_next]
        out_specs.append(pl.BlockSpec((FFN_TM, D_MODEL), row))
        out_shape.append(jax.ShapeDtypeStruct((n_tok, D_MODEL), BF16))
    return pl.pallas_call(
        functools.partial(_ffn_kernel, next_norm),
        grid=(n_tok // FFN_TM, F // tf),
        in_specs=in_specs,
        out_specs=tuple(out_specs),
        out_shape=tuple(out_shape),
        scratch_shapes=[pltpu.VMEM((FFN_TM, D_MODEL), F32)],
        compiler_params=_params("parallel", "arbitrary"),
        name="ffn_dense",
    )(*args)


def _combine_weights(logits):
    lane = lax.broadcasted_iota(jnp.int32, logits.shape, 1)
    lg = jnp.where(lane < N_EXPERTS, logits, -jnp.inf)
    m1 = jnp.max(lg, axis=-1, keepdims=True)
    i1 = jnp.min(jnp.where(lg == m1, lane, LANES), axis=-1, keepdims=True)
    lg2 = jnp.where(lane == i1, -jnp.inf, lg)
    m2 = jnp.max(lg2, axis=-1, keepdims=True)
    i2 = jnp.min(jnp.where(lg2 == m2, lane, LANES), axis=-1, keepdims=True)
    e2 = jnp.exp(m2 - m1)
    den = 1.0 + e2
    return jnp.where(lane == i1, 1.0 / den, 0.0) + jnp.where(lane == i2, e2 / den, 0.0)


def _moe_kernel(h_ref, lg_ref, wg_ref, wu_ref, wd_ref, x_ref, mod_ref, gn_ref, x2_ref, acc, comb):
    e = pl.program_id(1)
    f = pl.program_id(2)
    first = (e == 0) & (f == 0)
    last = (e == pl.num_programs(1) - 1) & (f == pl.num_programs(2) - 1)

    @pl.when(first)
    def _():
        comb[...] = _combine_weights(lg_ref[...])

    lane = lax.broadcasted_iota(jnp.int32, comb.shape, 1)
    ce = jnp.sum(jnp.where(lane == e, comb[...], 0.0), axis=-1, keepdims=True)
    h = h_ref[...]
    gate = jnp.dot(h, wg_ref[0].astype(BF16), preferred_element_type=F32)
    up = jnp.dot(h, wu_ref[0].astype(BF16), preferred_element_type=F32)
    part = ce * jnp.dot((_silu(gate) * up).astype(BF16), wd_ref[0].astype(BF16),
                        preferred_element_type=F32)

    @pl.when(first)
    def _():
        acc[...] = part

    @pl.when(jnp.logical_not(first))
    def _():
        acc[...] += part

    @pl.when(last)
    def _():
        for r in range(FFN_TM // ROW_TILE):
            rs = slice(r * ROW_TILE, (r + 1) * ROW_TILE)
            x2_ref[rs, :] = x_ref[rs, :] + mod_ref[r, 5:6, :] * _rms(acc[rs, :], gn_ref[3:4, :])


def moe_dense(h2, logits, wg, wu, wd, x1, modt, gn, tf):
    n_tok = x1.shape[0]
    E, _, F = wg.shape
    nr = FFN_TM // ROW_TILE
    row = lambda i, e, f: (i, 0)
    return pl.pallas_call(
        _moe_kernel,
        grid=(n_tok // FFN_TM, E, F // tf),
        in_specs=[pl.BlockSpec((FFN_TM, D_MODEL), row),
                  pl.BlockSpec((FFN_TM, LANES), row),
                  pl.BlockSpec((1, D_MODEL, tf), lambda i, e, f: (e, 0, f)),
                  pl.BlockSpec((1, D_MODEL, tf), lambda i, e, f: (e, 0, f)),
                  pl.BlockSpec((1, tf, D_MODEL), lambda i, e, f: (e, f, 0)),
                  pl.BlockSpec((FFN_TM, D_MODEL), row),
                  pl.BlockSpec((nr, 6, D_MODEL), lambda i, e, f: (i, 0, 0)),
                  pl.BlockSpec((4, D_MODEL), lambda i, e, f: (0, 0))],
        out_specs=pl.BlockSpec((FFN_TM, D_MODEL), row),
        out_shape=jax.ShapeDtypeStruct((n_tok, D_MODEL), F32),
        scratch_shapes=[pltpu.VMEM((FFN_TM, D_MODEL), F32), pltpu.VMEM((FFN_TM, LANES), F32)],
        compiler_params=_params("parallel", "arbitrary", "arbitrary"),
        name="moe_dense",
    )(h2, logits, wg, wu, wd, x1, modt, gn)


def _pack_w_in(w):
    d = w.shape[0]
    qw = N_Q_HEADS * HEAD_DIM
    kw = N_KV_HEADS * HEAD_DIM
    dw = DN_HEADS * DN_D
    o_k = qw
    o_v = o_k + kw
    o_qkv = o_v + kw
    o_z = o_qkv + 3 * dw
    o_ba = o_z + dw
    o_ga = o_ba + 4 * DN_HEADS
    o_gb = o_ga + D_MODEL
    pad = jnp.zeros((d, N_PROJ - w.shape[1]), w.dtype)
    return jnp.concatenate([w[:, :qw], w[:, o_ga:o_gb], w[:, o_gb:], w[:, o_z:o_ba], w[:, o_qkv:o_z],
                            w[:, o_k:o_v], w[:, o_v:o_qkv], w[:, o_ba:o_ga], pad], axis=1).astype(BF16)


def _lane_consts(p):
    out = jnp.zeros((DN_HEADS, 1, LANES), F32)
    out = out.at[:, 0, 2].set(p[0])
    return out.at[:, 0, 3].set(p[1])


def kernel(x_prompt, x_sample, c, cache_k_l0, cache_v_l0, state_delta_l0, cache_k_l1, cache_v_l1,
           state_delta_l1, c_ctx, w_ada, b_ada, norm_g, w_in, attn_sink, conv_w, dn_a_log, dn_dt_bias,
           dn_norm_g, w_pa, w_pb, w_o, w_ff_gate, w_ff_up, w_ff_down, w_router, w_e_gate, w_e_up,
           w_e_down):
    Bc, Tc, _ = x_prompt.shape
    Bl, Tl, _ = x_sample.shape
    n_ctx = Bc * Tc
    n_tok = n_ctx + Bl * Tl
    depth = w_in.shape[0]
    caches = ((cache_k_l0, cache_v_l0, state_delta_l0), (cache_k_l1, cache_v_l1, state_delta_l1))

    x = jnp.concatenate([x_prompt.reshape(n_ctx, D_MODEL), x_sample.reshape(Bl * Tl, D_MODEL)], axis=0)
    cond = jnp.zeros((SUBLANES, D_MODEL), F32).at[:Bl].set(c).at[Bl].set(c_ctx)
    mod = ada_mod(cond, w_ada, b_ada)
    tile_row = np.concatenate([np.full(n_ctx // ROW_TILE, Bl),
                               np.repeat(np.arange(Bl), Tl // ROW_TILE)]).astype(np.int32)
    modt = mod[:, tile_row].reshape(depth, n_tok // ROW_TILE, 6, D_MODEL)
    cos, sin = rope_tables(Tl)

    outs = []
    h = norm_mod(x, norm_g[0, 0:1], modt[0])
    for l in range(depth):
        proj = matmul(h, _pack_w_in(w_in[l]), 512, 1024)
        k_c, v_c, s_c = caches[l]
        oa_c = ctx_attention(proj, attn_sink[l], Bc, Tc)
        oa_l = lat_attention(proj, attn_sink[l], k_c, v_c, cos, sin, n_ctx, Bl, Tl)
        alog = _lane_consts(dn_a_log[l])
        dtb = _lane_consts(dn_dt_bias[l])
        gdn = dn_norm_g[l].reshape(1, DN_D)
        ob_c, s_fin = deltanet(proj, conv_w[l], alog, dtb, gdn, None, 0, Bc, Tc)
        ob_l = deltanet(proj, conv_w[l], alog, dtb, gdn, s_c, n_ctx, Bl, Tl)
        o_a = jnp.concatenate([oa_c, oa_l], axis=0)
        o_b = jnp.concatenate([ob_c, ob_l], axis=0)
        moe = l % 2 == 1
        i = l // 2
        wr = None
        if moe:
            wr = jnp.zeros((D_MODEL, LANES), F32).at[:, :N_EXPERTS].set(w_router[i])
        res = mix(o_a, o_b, proj, x, modt[l], norm_g[l], w_pa[l].astype(BF16), w_pb[l].astype(BF16),
                  w_o[l].astype(BF16), wr)
        if moe:
            x1, h2, logits = res
            x = moe_dense(h2, logits, w_e_gate[i], w_e_up[i], w_e_down[i], x1, modt[l], norm_g[l], 512)
            h = None if l + 1 == depth else norm_mod(x, norm_g[l + 1, 0:1], modt[l + 1])
        else:
            x1, h2 = res
            nxt = l + 1 < depth
            r = ffn_dense(h2, w_ff_gate[i].astype(BF16), w_ff_up[i].astype(BF16),
                          w_ff_down[i].astype(BF16), x1, modt[l], norm_g[l],
                          modt[l + 1] if nxt else None, norm_g[l + 1] if nxt else None, 1408)
            x, h = (r[0], r[1]) if nxt else (r[0], None)
        k_out = proj[:n_ctx, C_K:C_K + N_KV_HEADS * HEAD_DIM].reshape(Bc, Tc, N_KV_HEADS, HEAD_DIM)
        v_out = proj[:n_ctx, C_V:C_V + N_KV_HEADS * HEAD_DIM].reshape(Bc, Tc, N_KV_HEADS, HEAD_DIM)
        outs.append((k_out, v_out, s_fin))

    y_prompt = x[:n_ctx].reshape(Bc, Tc, D_MODEL)
    y_sample = x[n_ctx:].reshape(Bl, Tl, D_MODEL)
    return (y_prompt, y_sample, outs[0][0], outs[0][1], outs[0][2], outs[1][0], outs[1][1], outs[1][2])
```

```python
import functools

import numpy as np
import jax
import jax.numpy as jnp
from jax import lax
from jax.experimental import pallas as pl
from jax.experimental.pallas import tpu as pltpu
from jax.experimental.pallas import tpu_sc as plsc

F32 = jnp.float32
BF16 = jnp.bfloat16
HIGHEST = lax.Precision.HIGHEST

D_MODEL = 1024
HEAD_DIM = 128
N_Q_HEADS = 8
N_KV_HEADS = 2
GQA = N_Q_HEADS // N_KV_HEADS
WINDOW = 128
GRID_W = 64
ROPE_BASE = 10000.0
DN_HEADS = 8
DN_D = 128
DN_CONV = 5
DN_CHUNK = 256
DN_LEVELS = (16, 64)
DN_GROUP = 4
N_EXPERTS = 8
TOP_K = 2
EPS = 1e-6
NEG_INF = -1e30

LANES = 128
SUBLANES = 8

ROW_TILE = 256
FFN_TM = 512
MOE_TM = 512
SC_WINDOW = 32
SC_CORES = 2
SC_SUBCORES = 16

C_Q = 0
C_GA = 1024
C_GB = 2048
C_Z = 3072
C_QB = 4096
C_KB = 5120
C_VB = 6144
C_K = 7168
C_V = 7424
C_BA = 7680
N_PROJ = 8192

VMEM_LIMIT = 56 * 1024 * 1024


def _params(*sem):
    return pltpu.CompilerParams(dimension_semantics=sem, vmem_limit_bytes=VMEM_LIMIT)


def _rms(x, g):
    return x * lax.rsqrt(jnp.mean(x * x, axis=-1, keepdims=True) + EPS) * g


def _silu(x):
    return x * jax.nn.sigmoid(x)


def _bdot(a, b):
    return jnp.dot(a.astype(BF16), b.astype(BF16), preferred_element_type=F32)


def _bdot_nt(a, b):
    return lax.dot_general(a.astype(BF16), b.astype(BF16), (((1,), (1,)), ((), ())),
                           preferred_element_type=F32)


def _fdot(a, b):
    return jnp.dot(a, b, precision=HIGHEST, preferred_element_type=F32)


def _dot_split_rhs(a, b):
    b1 = b.astype(BF16)
    r1 = b - b1.astype(F32)
    b2 = r1.astype(BF16)
    b3 = (r1 - b2.astype(F32)).astype(BF16)
    dot = functools.partial(jnp.dot, preferred_element_type=F32)
    return dot(a, b1) + dot(a, b2) + dot(a, b3)


def _ada_kernel(c_ref, w_ref, b_ref, o_ref):
    o_ref[0] = _fdot(_silu(c_ref[...]), w_ref[0]) + b_ref[0]


def ada_mod(cond, w_ada, b_ada):
    L = w_ada.shape[0]
    n = w_ada.shape[2] // D_MODEL
    return pl.pallas_call(
        _ada_kernel,
        grid=(L, n),
        in_specs=[pl.BlockSpec((SUBLANES, D_MODEL), lambda l, j: (0, 0)),
                  pl.BlockSpec((1, D_MODEL, D_MODEL), lambda l, j: (l, 0, j)),
                  pl.BlockSpec((1, 1, D_MODEL), lambda l, j: (l, 0, j))],
        out_specs=pl.BlockSpec((1, SUBLANES, D_MODEL), lambda l, j: (l, 0, j)),
        out_shape=jax.ShapeDtypeStruct((L, SUBLANES, 6 * D_MODEL), F32),
        compiler_params=_params("parallel", "parallel"),
        name="ada_mod",
    )(cond, w_ada, b_ada.reshape(L, 1, 6 * D_MODEL))


def _norm_mod_kernel(x_ref, g_ref, mod_ref, h_ref):
    h = _rms(x_ref[...], g_ref[...]) * (1.0 + mod_ref[0, 1:2, :]) + mod_ref[0, 0:1, :]
    h_ref[...] = h.astype(h_ref.dtype)


def norm_mod(x, g, modt):
    n = x.shape[0] // ROW_TILE
    return pl.pallas_call(
        _norm_mod_kernel,
        grid=(n,),
        in_specs=[pl.BlockSpec((ROW_TILE, D_MODEL), lambda i: (i, 0)),
                  pl.BlockSpec((1, D_MODEL), lambda i: (0, 0)),
                  pl.BlockSpec((1, 6, D_MODEL), lambda i: (i, 0, 0))],
        out_specs=pl.BlockSpec((ROW_TILE, D_MODEL), lambda i: (i, 0)),
        out_shape=jax.ShapeDtypeStruct(x.shape, BF16),
        compiler_params=_params("parallel"),
        name="norm_mod",
    )(x, g, modt)


def _mm_kernel(a_ref, b_ref, o_ref):
    o_ref[...] = jnp.dot(a_ref[...], b_ref[...], preferred_element_type=F32).astype(o_ref.dtype)


def matmul(a, b, tm, tn, out_dtype=F32):
    M, K = a.shape
    N = b.shape[1]
    return pl.pallas_call(
        _mm_kernel,
        grid=(N // tn, M // tm),
        in_specs=[pl.BlockSpec((tm, K), lambda j, i: (i, 0)),
                  pl.BlockSpec((K, tn), lambda j, i: (0, j))],
        out_specs=pl.BlockSpec((tm, tn), lambda j, i: (i, j)),
        out_shape=jax.ShapeDtypeStruct((M, N), out_dtype),
        compiler_params=_params("parallel", "parallel"),
        name="in_proj",
    )(a, b)


def _ctx_attn_kernel(sink_ref, q_ref, k_ref, v_ref, o_ref):
    hk = pl.program_id(1)
    k = k_ref[...].astype(BF16)
    v = v_ref[...].astype(BF16)
    scale = HEAD_DIM ** -0.5
    for g in range(GQA):
        q = q_ref[:, g * HEAD_DIM:(g + 1) * HEAD_DIM]
        s = _bdot_nt(q, k) * scale
        sink = sink_ref[hk * GQA + g]
        m = jnp.maximum(jnp.max(s, axis=-1, keepdims=True), sink)
        p = jnp.exp(s - m)
        den = jnp.sum(p, axis=-1, keepdims=True) + jnp.exp(sink - m)
        o = _bdot(p, v) / den
        o_ref[:, g * HEAD_DIM:(g + 1) * HEAD_DIM] = o.astype(o_ref.dtype)


def ctx_attention(proj, sink, n_seq, T):
    qw = GQA * HEAD_DIM
    return pl.pallas_call(
        _ctx_attn_kernel,
        grid_spec=pltpu.PrefetchScalarGridSpec(
            num_scalar_prefetch=1,
            grid=(n_seq, N_KV_HEADS),
            in_specs=[pl.BlockSpec((T, qw), lambda b, h, s: (b, C_Q // qw + h)),
                      pl.BlockSpec((T, HEAD_DIM), lambda b, h, s: (b, C_K // HEAD_DIM + h)),
                      pl.BlockSpec((T, HEAD_DIM), lambda b, h, s: (b, C_V // HEAD_DIM + h))],
            out_specs=pl.BlockSpec((T, qw), lambda b, h, s: (b, h))),
        out_shape=jax.ShapeDtypeStruct((n_seq * T, N_Q_HEADS * HEAD_DIM), BF16),
        compiler_params=_params("parallel", "parallel"),
        name="ctx_attn",
    )(sink, proj, proj, proj)


def _rope(x, cos, sin):
    lane = lax.broadcasted_iota(jnp.int32, x.shape, 1)
    q = HEAD_DIM // 4
    swapped = jnp.where((lane % (2 * q)) < q,
                        pltpu.roll(x, HEAD_DIM - q, 1), pltpu.roll(x, q, 1))
    return x * cos + swapped * sin


def _lat_attn_kernel(T, sink_ref, q_ref, k_ref, v_ref, kc_ref, vc_ref, cosq_ref, sinq_ref,
                     cos_ref, sin_ref, o_ref, kpad, vpad):
    hk = pl.program_id(1)
    i = pl.program_id(2)
    B = WINDOW

    @pl.when(i == 0)
    def _():
        zeros = jnp.zeros((B, HEAD_DIM), BF16)
        kpad[0:B, :] = zeros
        kpad[T + B:T + 2 * B, :] = zeros
        vpad[0:B, :] = zeros
        vpad[T + B:T + 2 * B, :] = zeros
        kpad[B:T + B, :] = _rope(k_ref[...], cos_ref[...], sin_ref[...]).astype(BF16)
        vpad[B:T + B, :] = v_ref[...].astype(BF16)

    start = pl.multiple_of(i * B, B)
    kw = kpad[pl.ds(start, 3 * B), :]
    vw = vpad[pl.ds(start, 3 * B), :]
    kc = kc_ref[0].astype(BF16)
    vc = vc_ref[0].astype(BF16)

    rows = GQA * B
    qi = lax.broadcasted_iota(jnp.int32, (rows, 3 * B), 0) % B
    kj = lax.broadcasted_iota(jnp.int32, (rows, 3 * B), 1) - B
    kpos = kj + i * B
    mask = (jnp.abs(qi - kj) <= WINDOW) & (kpos >= 0) & (kpos < T)

    cq = cosq_ref[...]
    sq = sinq_ref[...]
    q = jnp.concatenate(
        [_rope(q_ref[:, g * HEAD_DIM:(g + 1) * HEAD_DIM], cq, sq) for g in range(GQA)], axis=0)
    scale = HEAD_DIM ** -0.5
    s_c = _bdot_nt(q, kc) * scale
    s_w = jnp.where(mask, _bdot_nt(q, kw) * scale, NEG_INF)
    gi = lax.broadcasted_iota(jnp.int32, (rows, 1), 0) // B
    sink = jnp.zeros((rows, 1), F32)
    for g in range(GQA):
        sink = jnp.where(gi == g, sink_ref[hk * GQA + g], sink)
    m = jnp.maximum(jnp.maximum(jnp.max(s_c, axis=-1, keepdims=True),
                                jnp.max(s_w, axis=-1, keepdims=True)), sink)
    p_c = jnp.exp(s_c - m)
    p_w = jnp.exp(s_w - m)
    den = (jnp.sum(p_c, axis=-1, keepdims=True) + jnp.sum(p_w, axis=-1, keepdims=True)
           + jnp.exp(sink - m))
    o = (_bdot(p_c, vc) + _bdot(p_w, vw)) / den
    for g in range(GQA):
        o_ref[:, g * HEAD_DIM:(g + 1) * HEAD_DIM] = o[g * B:(g + 1) * B, :].astype(o_ref.dtype)


def lat_attention(proj, sink, cache_k, cache_v, cos, sin, row0, n_seq, T):
    qw = GQA * HEAD_DIM
    B = WINDOW
    nb = T // B
    P = cache_k.shape[1]
    kc = cache_k.reshape(n_seq, P, N_KV_HEADS * HEAD_DIM)
    vc = cache_v.reshape(n_seq, P, N_KV_HEADS * HEAD_DIM)
    rb = row0 // B
    rt = row0 // T
    return pl.pallas_call(
        functools.partial(_lat_attn_kernel, T),
        grid_spec=pltpu.PrefetchScalarGridSpec(
            num_scalar_prefetch=1,
            grid=(n_seq, N_KV_HEADS, nb),
            in_specs=[pl.BlockSpec((B, qw), lambda b, h, i, s: (rb + b * nb + i, C_Q // qw + h)),
                      pl.BlockSpec((T, HEAD_DIM), lambda b, h, i, s: (rt + b, C_K // HEAD_DIM + h)),
                      pl.BlockSpec((T, HEAD_DIM), lambda b, h, i, s: (rt + b, C_V // HEAD_DIM + h)),
                      pl.BlockSpec((1, P, HEAD_DIM), lambda b, h, i, s: (b, 0, h)),
                      pl.BlockSpec((1, P, HEAD_DIM), lambda b, h, i, s: (b, 0, h)),
                      pl.BlockSpec((B, HEAD_DIM), lambda b, h, i, s: (i, 0)),
                      pl.BlockSpec((B, HEAD_DIM), lambda b, h, i, s: (i, 0)),
                      pl.BlockSpec((T, HEAD_DIM), lambda b, h, i, s: (0, 0)),
                      pl.BlockSpec((T, HEAD_DIM), lambda b, h, i, s: (0, 0))],
            out_specs=pl.BlockSpec((B, qw), lambda b, h, i, s: (b * nb + i, h)),
            scratch_shapes=[pltpu.VMEM((T + 2 * B, HEAD_DIM), BF16),
                            pltpu.VMEM((T + 2 * B, HEAD_DIM), BF16)]),
        out_shape=jax.ShapeDtypeStruct((n_seq * T, N_Q_HEADS * HEAD_DIM), BF16),
        compiler_params=_params("parallel", "parallel", "arbitrary"),
        name="lat_attn",
    )(sink, proj, proj, proj, kc, vc, cos, sin, cos, sin)


def rope_tables(T):
    rows = (np.arange(T) // GRID_W).astype(np.float32)
    cols = (np.arange(T) % GRID_W).astype(np.float32)
    n_freq = HEAD_DIM // 4
    inv = jnp.asarray(ROPE_BASE, F32) ** (-jnp.arange(n_freq, dtype=F32) / n_freq)
    ar = jnp.asarray(rows)[:, None] * inv
    ac = jnp.asarray(cols)[:, None] * inv
    cos = jnp.concatenate([jnp.cos(ar), jnp.cos(ar), jnp.cos(ac), jnp.cos(ac)], axis=-1)
    sin = jnp.concatenate([-jnp.sin(ar), jnp.sin(ar), -jnp.sin(ac), jnp.sin(ac)], axis=-1)
    return cos, sin


def _dn_kernel(T, hps, has_s0, *refs):
    n_in = 12 if has_s0 else 11
    (q_ref, k_ref, v_ref, z_ref, ba_ref, wq_ref, wk_ref, wv_ref, alog_ref, dt_ref, gn_ref) = refs[:11]
    s0_ref = refs[11] if has_s0 else None
    if has_s0:
        o_ref = refs[n_in]
        sfin_ref = None
        scratch = refs[n_in + 1:]
    else:
        o_ref, sfin_ref = refs[n_in], refs[n_in + 1]
        scratch = refs[n_in + 2:]
    xpad, qs, ks, vs, cols, gcs, gcT, kTs, U, WQ, QKM, KDT, EGL, OST = scratch

    hg = pl.program_id(1)
    C = DN_CHUNK
    D = DN_D
    n = T // C
    half = DN_CONV // 2
    PAD = SUBLANES

    zpad = jnp.zeros((PAD, D), F32)
    xpad[0:PAD, :] = zpad
    xpad[T + PAD:T + 2 * PAD, :] = zpad

    def conv_silu(x_ref, w_ref, hh):
        xpad[PAD:T + PAD, :] = x_ref[:, hh * D:(hh + 1) * D]
        w = w_ref[:, hh * D:(hh + 1) * D]
        acc = xpad[pl.ds(PAD - half, T), :] * w[0:1, :]
        for j in range(1, DN_CONV):
            acc = acc + xpad[pl.ds(PAD - half + j, T), :] * w[j:j + 1, :]
        return _silu(acc)

    def l2n(x):
        return x * lax.rsqrt(jnp.sum(x * x, axis=-1, keepdims=True) + EPS)

    ri = lax.broadcasted_iota(jnp.int32, (C, C), 0)
    ci = lax.broadcasted_iota(jnp.int32, (C, C), 1)
    eye = jnp.where(ri == ci, 1.0, 0.0).astype(F32)
    same_blk = {s: (ri // s) == (ci // s) for s in DN_LEVELS}
    lower =jnp.where(ri >= ci, 1.0, 0.0).astype(BF16)
    li = lax.broadcasted_iota(jnp.int32, (LANES, LANES), 0)
    lj = lax.broadcasted_iota(jnp.int32, (LANES, LANES), 1)
    lane_t = lax.broadcasted_iota(jnp.int32, (T, LANES), 1)
    lane_c = lax.broadcasted_iota(jnp.int32, (C, LANES), 1)

    for hh in range(hps):
        qs[hh] = l2n(conv_silu(q_ref, wq_ref, hh)) * (D ** -0.5)
        ks[hh] = l2n(conv_silu(k_ref, wk_ref, hh))
        vs[hh] = conv_silu(v_ref, wv_ref, hh)
        head = hg * hps + hh
        sel = jnp.where((lj < 4) & (li == lj * DN_HEADS + head), 1.0, 0.0).astype(F32)
        raw = _fdot(ba_ref[...], sel)
        gate = -jnp.exp(alog_ref[hh]) * jax.nn.softplus(raw + dt_ref[hh])
        cols[hh] = jnp.where(lane_t < 2, jax.nn.sigmoid(raw), gate)
        for c in range(n):
            rows = slice(c * C, (c + 1) * C)
            g = cols[hh, rows, :]
            pre = _dot_split_rhs(lower, g)
            gc = jnp.where(lane_c == 3, pre[C - 1:C, :] - pre + g, pre)
            gcs[hh, rows, :] = gc
            gcT[hh * n + c] = gc.T[0:SUBLANES, :]
            kTs[hh * n + c] = ks[hh, rows, :].T

    chains = [(hh, c, d) for hh in range(hps) for c in range(n) for d in (0, 1)]
    chain_id = {ch: j for j, ch in enumerate(chains)}

    def nilpotent_inverse(ms, squarings):
        ts = [eye + m for m in ms]
        pws = ms
        for _ in range(squarings):
            pws = [_bdot(pw, pw) for pw in pws]
            ts = [t + _bdot(t, pw) for t, pw in zip(ts, pws)]
        return ts

    def phase_a(j0, grp):
        G = range(len(grp))
        rows = [slice(c * C, (c + 1) * C) for (_, c, _) in grp]
        incl = [(ri >= ci) if d == 0 else (ri <= ci) for (_, _, d) in grp]
        strict = [(ri > ci) if d == 0 else (ri < ci) for (_, _, d) in grp]
        q = [qs[grp[i][0], rows[i], :] for i in G]
        k = [ks[grp[i][0], rows[i], :] for i in G]
        v = [vs[grp[i][0], rows[i], :] for i in G]
        gcc = [gcs[hh, rows[i], 2 + d:3 + d] for i, (hh, _, d) in enumerate(grp)]
        bcol = [cols[hh, rows[i], d:d + 1] for i, (hh, _, d) in enumerate(grp)]
        kT = [kTs[hh * n + c] for (hh, c, _) in grp]
        gcr = [gcT[hh * n + c, 2 + d:3 + d, :] for (hh, c, d) in grp]
        g_last = [gcr[i][:, C - 1:C] if grp[i][2] == 0 else gcr[i][:, 0:1] for i in G]
        decay = [jnp.where(incl[i], jnp.exp(jnp.where(incl[i], gcc[i] - gcr[i], 0.0)), 0.0) for i in G]
        qkk = [_bdot(jnp.concatenate([q[i], k[i]], axis=0), kT[i]) for i in G]
        a = [jnp.where(strict[i], qkk[i][C:2 * C, :] * bcol[i] * decay[i], 0.0) for i in G]
        sizes = DN_LEVELS + (C,)
        t_inv = nilpotent_inverse([jnp.where(same_blk[sizes[0]], -a[i], 0.0) for i in G],
                                  sizes[0].bit_length() - 2)
        for prev, size in zip(sizes[:-1], sizes[1:]):
            off = [jnp.where(same_blk[prev], 0.0, -a[i]) for i in G]
            if size < C:
                off = [jnp.where(same_blk[size], off[i], 0.0) for i in G]
            e_neg = [_bdot(t_inv[i], off[i]) for i in G]
            f = nilpotent_inverse(e_neg, (size // prev).bit_length() - 2)
            t_inv = [_bdot(f[i], t_inv[i]) for i in G]
        egc = [jnp.exp(gcc[i]) for i in G]
        uw = [_bdot(t_inv[i], jnp.concatenate([v[i] * bcol[i], k[i] * (bcol[i] * egc[i])], axis=1))
              for i in G]
        r0, r1 = j0, j0 + len(grp)
        U[r0 * C:r1 * C, :] = jnp.concatenate([uw[i][:, 0:D] for i in G], axis=0)
        WQ[r0 * 2 * C:r1 * 2 * C, :] = jnp.concatenate(
            [jnp.concatenate([uw[i][:, D:2 * D], q[i] * egc[i]], axis=0) for i in G], axis=0).astype(BF16)
        QKM[r0 * C:r1 * C, :] = jnp.concatenate(
            [qkk[i][0:C, :] * decay[i] for i in G], axis=0).astype(BF16)
        KDT[r0 * D:r1 * D, :] = jnp.concatenate(
            [kT[i] * jnp.exp(g_last[i] - gcr[i]) for i in G], axis=0).astype(BF16)
        EGL[r0 * SUBLANES:r1 * SUBLANES, :] = jnp.concatenate(
            [jnp.broadcast_to(jnp.exp(g_last[i]), (SUBLANES, LANES)) for i in G], axis=0)

    for j0 in range(0, len(chains), DN_GROUP):
        phase_a(j0, chains[j0:j0 + DN_GROUP])

    if has_s0:
        state = [s0_ref[0, d, hh] for hh in range(hps) for d in (0, 1)]
    else:
        state = [jnp.zeros((D, D), F32) for _ in range(2 * hps)]
    for i in range(n):
        outs = []
        new_state = []
        for hh in range(hps):
            for d in (0, 1):
                j = chain_id[(hh, i if d == 0 else n - 1 - i, d)]
                s_old = state[hh * 2 + d]
                xs = jnp.dot(WQ[j * 2 * C:(j + 1) * 2 * C, :], s_old.astype(BF16),
                             preferred_element_type=F32)
                v_new = (U[j * C:(j + 1) * C, :] - xs[0:C, :]).astype(BF16)
                outs.append(xs[C:2 * C, :] + jnp.dot(QKM[j * C:(j + 1) * C, :], v_new,
                                                     preferred_element_type=F32))
                new_state.append(s_old * EGL[j * SUBLANES:j * SUBLANES + 1, 0:1]
                                 + jnp.dot(KDT[j * D:(j + 1) * D, :], v_new, preferred_element_type=F32))
        OST[i] = jnp.concatenate(outs, axis=0)
        state = new_state

    if not has_s0:
        for hh in range(hps):
            for d in (0, 1):
                sfin_ref[0, d, hh] = state[hh * 2 + d]

    for hh in range(hps):
        for c in range(n):
            rows = slice(c * C, (c + 1) * C)
            o = (OST[c, (2 * hh) * C:(2 * hh + 1) * C, :]
                 + OST[n - 1 - c, (2 * hh + 1) * C:(2 * hh + 2) * C, :])
            y = _rms(o, gn_ref[...]) * _silu(z_ref[rows, hh * D:(hh + 1) * D])
            o_ref[rows, hh * D:(hh + 1) * D] = y.astype(o_ref.dtype)


def deltanet(proj, conv_w, alog_l, dt_l, gn, s0, row0, n_seq, T, hps):
    rt = row0 // T
    has_s0 = s0 is not None
    W = hps * DN_D
    n = T // DN_CHUNK
    n_chain = hps * n * 2

    def col(c0):
        return pl.BlockSpec((T, W), lambda b, h: (rt + b, c0 // W + h))

    def wcol(c0):
        return pl.BlockSpec((DN_CONV, W), lambda b, h: (0, c0 // W + h))

    in_specs = [col(C_QB), col(C_KB), col(C_VB), col(C_Z),
                pl.BlockSpec((T, LANES), lambda b, h: (rt + b, C_BA // LANES)),
                wcol(0), wcol(DN_HEADS * DN_D), wcol(2 * DN_HEADS * DN_D),
                pl.BlockSpec((hps, 1, LANES), lambda b, h: (h, 0, 0)),
                pl.BlockSpec((hps, 1, LANES), lambda b, h: (h, 0, 0)),
                pl.BlockSpec((1, DN_D), lambda b, h: (0, 0))]
    args = [proj, proj, proj, proj, proj, conv_w, conv_w, conv_w, alog_l, dt_l, gn]
    o_spec = pl.BlockSpec((T, W), lambda b, h: (b, h))
    o_shape = jax.ShapeDtypeStruct((n_seq * T, DN_HEADS * DN_D), BF16)
    s_spec = pl.BlockSpec((1, 2, hps, DN_D, DN_D), lambda b, h: (b, 0, h, 0, 0))
    if has_s0:
        in_specs.append(s_spec)
        args.append(s0)
        out_specs, out_shape = o_spec, o_shape
    else:
        out_specs = (o_spec, s_spec)
        out_shape = (o_shape, jax.ShapeDtypeStruct((n_seq, 2, DN_HEADS, DN_D, DN_D), F32))
    C = DN_CHUNK
    scratch = [pltpu.VMEM((T + 2 * SUBLANES, DN_D), F32),
               pltpu.VMEM((hps, T, DN_D), F32),
               pltpu.VMEM((hps, T, DN_D), F32),
               pltpu.VMEM((hps, T, DN_D), F32),
               pltpu.VMEM((hps, T, LANES), F32),
               pltpu.VMEM((hps, T, LANES), F32),
               pltpu.VMEM((hps * n, SUBLANES, C), F32),
               pltpu.VMEM((hps * n, DN_D, C), F32),
               pltpu.VMEM((n_chain * C, DN_D), F32),
               pltpu.VMEM((n_chain * 2 * C, DN_D), BF16),
               pltpu.VMEM((n_chain * C, C), BF16),
               pltpu.VMEM((n_chain * DN_D, C), BF16),
               pltpu.VMEM((n_chain * SUBLANES, LANES), F32),
               pltpu.VMEM((n, 2 * hps * C, DN_D), F32)]
    return pl.pallas_call(
        functools.partial(_dn_kernel, T, hps, has_s0),
        grid=(n_seq, DN_HEADS // hps),
        in_specs=in_specs,
        out_specs=out_specs,
        out_shape=out_shape,
        scratch_shapes=scratch,
        compiler_params=_params("parallel", "parallel"),
        name="deltanet_lat" if has_s0 else "deltanet_ctx",
    )(*args)


def _route_top2(logits):
    lane = lax.broadcasted_iota(jnp.int32, logits.shape, 1).astype(F32)
    lg = jnp.where(lane < N_EXPERTS, logits, -jnp.inf)
    m1 = jnp.max(lg, axis=-1, keepdims=True)
    i1 = jnp.min(jnp.where(lg == m1, lane, float(LANES)), axis=-1, keepdims=True)
    lg2 = jnp.where(lane == i1, -jnp.inf, lg)
    m2 = jnp.max(lg2, axis=-1, keepdims=True)
    i2 = jnp.min(jnp.where(lg2 == m2, lane, float(LANES)), axis=-1, keepdims=True)
    e2 = jnp.exp(m2 - m1)
    den = 1.0 + e2
    return jnp.where(lane == 0.0, i1, jnp.where(lane == 1.0, i2,
                     jnp.where(lane == 2.0, 1.0 / den, jnp.where(lane == 3.0, e2 / den, 0.0))))


def _mix_kernel(route, n_ctx_tiles, oac_ref, oal_ref, obc_ref, obl_ref, ga_ref, gb_ref, x_ref,
                mod_ref, gn_ref, wpa_ref, wpb_ref, wo_ref, *rest):
    if route:
        wr_ref, x1_ref, h2_ref, rt_ref = rest
    else:
        x1_ref, h2_ref = rest
    is_ctx = pl.program_id(0) < n_ctx_tiles
    o_a = jnp.where(is_ctx, oac_ref[...], oal_ref[...])
    o_b = jnp.where(is_ctx, obc_ref[...], obl_ref[...])
    br_a = jnp.dot(o_a, wpa_ref[...], preferred_element_type=F32)
    br_b = jnp.dot(o_b, wpb_ref[...], preferred_element_type=F32)
    merged = jax.nn.sigmoid(ga_ref[...]) * br_a + jax.nn.sigmoid(gb_ref[...]) * br_b
    y = jnp.dot(merged.astype(BF16), wo_ref[...], preferred_element_type=F32)
    x1 = x_ref[...] + mod_ref[0, 2:3, :] * _rms(y, gn_ref[1:2, :])
    x1_ref[...] = x1
    h2 = _rms(x1, gn_ref[2:3, :]) * (1.0 + mod_ref[0, 4:5, :]) + mod_ref[0, 3:4, :]
    h2_ref[...] = h2.astype(h2_ref.dtype)
    if route:
        rt_ref[...] = _route_top2(_fdot(h2, wr_ref[...]))


def mix(oa_c, oa_l, ob_c, ob_l, proj, x, modt, gn, wpa, wpb, wo, w_router):
    n_tok = x.shape[0]
    n = n_tok // ROW_TILE
    nc = oa_c.shape[0] // ROW_TILE
    nl = oa_l.shape[0] // ROW_TILE
    route = w_router is not None
    row = lambda i: (i, 0)
    full = lambda i: (0, 0)
    crow = lambda i: (jnp.minimum(i, nc - 1), 0)
    lrow = lambda i: (jnp.maximum(i - nc, 0), 0)
    in_specs = [pl.BlockSpec((ROW_TILE, D_MODEL), crow),
                pl.BlockSpec((ROW_TILE, D_MODEL), lrow),
                pl.BlockSpec((ROW_TILE, D_MODEL), crow),
                pl.BlockSpec((ROW_TILE, D_MODEL), lrow),
                pl.BlockSpec((ROW_TILE, D_MODEL), lambda i: (i, C_GA // D_MODEL)),
                pl.BlockSpec((ROW_TILE, D_MODEL), lambda i: (i, C_GB // D_MODEL)),
                pl.BlockSpec((ROW_TILE, D_MODEL), row),
                pl.BlockSpec((1, 6, D_MODEL), lambda i: (i, 0, 0)),
                pl.BlockSpec((4, D_MODEL), full),
                pl.BlockSpec((D_MODEL, D_MODEL), full),
                pl.BlockSpec((D_MODEL, D_MODEL), full),
                pl.BlockSpec((D_MODEL, D_MODEL), full)]
    args = [oa_c, oa_l, ob_c, ob_l, proj, proj, x, modt, gn, wpa, wpb, wo]
    out_specs = [pl.BlockSpec((ROW_TILE, D_MODEL), row), pl.BlockSpec((ROW_TILE, D_MODEL), row)]
    out_shape = [jax.ShapeDtypeStruct((n_tok, D_MODEL), F32),
                 jax.ShapeDtypeStruct((n_tok, D_MODEL), F32 if route else BF16)]
    if route:
        in_specs.append(pl.BlockSpec((D_MODEL, LANES), full))
        args.append(w_router)
        out_specs.append(pl.BlockSpec((ROW_TILE, LANES), row))
        out_shape.append(jax.ShapeDtypeStruct((n_tok, LANES), F32))
    assert nc + nl == n
    return pl.pallas_call(
        functools.partial(_mix_kernel, route, nc),
        grid=(n,),
        in_specs=in_specs,
        out_specs=tuple(out_specs),
        out_shape=tuple(out_shape),
        compiler_params=_params("parallel"),
        name="mix_route" if route else "mix",
    )(*args)


def _ffn_kernel(next_norm, h_ref, wg_ref, wu_ref, wd_ref, x_ref, mod_ref, gn_ref, *rest):
    if next_norm:
        modn_ref, gnn_ref, x2_ref, hn_ref, acc = rest
    else:
        x2_ref, acc = rest
    f = pl.program_id(1)
    h = h_ref[...]
    gate = jnp.dot(h, wg_ref[...], preferred_element_type=F32)
    up = jnp.dot(h, wu_ref[...], preferred_element_type=F32)
    part = jnp.dot((_silu(gate) * up).astype(BF16), wd_ref[...], preferred_element_type=F32)

    @pl.when(f == 0)
    def _():
        acc[...] = part

    @pl.when(f > 0)
    def _():
        acc[...] += part

    @pl.when(f == pl.num_programs(1) - 1)
    def _():
        for r in range(FFN_TM // ROW_TILE):
            rs = slice(r * ROW_TILE, (r + 1) * ROW_TILE)
            x2 = x_ref[rs, :] + mod_ref[r, 5:6, :] * _rms(acc[rs, :], gn_ref[3:4, :])
            x2_ref[rs, :] = x2
            if next_norm:
                hn = _rms(x2, gnn_ref[0:1, :]) * (1.0 + modn_ref[r, 1:2, :]) + modn_ref[r, 0:1, :]
                hn_ref[rs, :] = hn.astype(hn_ref.dtype)


def ffn_dense(h2, wg, wu, wd, x1, modt, gn, modt_next, gn_next, tf):
    n_tok = x1.shape[0]
    F = wg.shape[1]
    nr = FFN_TM // ROW_TILE
    next_norm = modt_next is not None
    row = lambda i, f: (i, 0)
    in_specs = [pl.BlockSpec((FFN_TM, D_MODEL), row),
                pl.BlockSpec((D_MODEL, tf), lambda i, f: (0, f)),
                pl.BlockSpec((D_MODEL, tf), lambda i, f: (0, f)),
                pl.BlockSpec((tf, D_MODEL), lambda i, f: (f, 0)),
                pl.BlockSpec((FFN_TM, D_MODEL), row),
                pl.BlockSpec((nr, 6, D_MODEL), lambda i, f: (i, 0, 0)),
                pl.BlockSpec((4, D_MODEL), lambda i, f: (0, 0))]
    args = [h2, wg, wu, wd, x1, modt, gn]
    out_specs = [pl.BlockSpec((FFN_TM, D_MODEL), row)]
    out_shape = [jax.ShapeDtypeStruct((n_tok, D_MODEL), F32)]
    if next_norm:
        in_specs += [pl.BlockSpec((nr, 6, D_MODEL), lambda i, f: (i, 0, 0)),
                     pl.BlockSpec((4, D_MODEL), lambda i, f: (0, 0))]
        args += [modt_next, gn_next]
        out_specs.append(pl.BlockSpec((FFN_TM, D_MODEL), row))
        out_shape.append(jax.ShapeDtypeStruct((n_tok, D_MODEL), BF16))
    return pl.pallas_call(
        functools.partial(_ffn_kernel, next_norm),
        grid=(n_tok // FFN_TM, F // tf),
        in_specs=in_specs,
        out_specs=tuple(out_specs),
        out_shape=tuple(out_shape),
        scratch_shapes=[pltpu.VMEM((FFN_TM, D_MODEL), F32)],
        compiler_params=_params("parallel", "arbitrary"),
        name="ffn_dense",
    )(*args)


def gather_rows(x, idx):
    n = idx.shape[0]
    w = x.shape[1]
    per_worker = n // (SC_CORES * SC_SUBCORES)
    steps = per_worker // SC_WINDOW
    assert steps * SC_WINDOW * SC_CORES * SC_SUBCORES == n
    mesh = plsc.VectorSubcoreMesh(core_axis_name="core", subcore_axis_name="subcore")

    @functools.partial(pl.kernel, out_type=jax.ShapeDtypeStruct((n, w), x.dtype), mesh=mesh,
                       scratch_types=[pltpu.VMEM((SC_WINDOW,), jnp.int32),
                                      pltpu.VMEM((SC_WINDOW, w), x.dtype),
                                      pltpu.SemaphoreType.DMA])
    def gather_kernel(x_hbm, i_hbm, o_hbm, idx_v, rows_v, sem):
        worker = lax.axis_index("subcore") * SC_CORES + lax.axis_index("core")
        base = worker * per_worker

        @pl.loop(0, steps)
        def _(s):
            off = base + s * SC_WINDOW
            pltpu.sync_copy(i_hbm.at[pl.ds(off, SC_WINDOW)], idx_v)
            pltpu.async_copy(x_hbm.at[idx_v], rows_v, sem).wait()
            pltpu.sync_copy(rows_v, o_hbm.at[pl.ds(off, SC_WINDOW)])

    return gather_kernel(x, idx)


def route_plan(route, tm):
    n_tok = route.shape[0]
    ef = route[:, 0:TOP_K].astype(jnp.int32).T.reshape(-1)
    onehot = (ef[:, None] == jnp.arange(N_EXPERTS, dtype=jnp.int32)[None, :]).astype(jnp.int32)
    csum = jnp.cumsum(onehot, axis=0)
    rank = jnp.sum(csum * onehot, axis=1) - 1
    counts = csum[-1]
    padded = ((counts + tm - 1) // tm) * tm
    ends = jnp.cumsum(padded)
    pos = (ends - padded)[ef] + rank
    n_rows = TOP_K * n_tok + N_EXPERTS * tm
    tok = jnp.tile(jnp.arange(n_tok, dtype=jnp.int32), TOP_K)
    src = jnp.zeros((n_rows,), jnp.int32).at[pos].set(tok)
    n_tiles = n_rows // tm
    n_used = (ends[-1] // tm).astype(jnp.int32)
    tile_start = jnp.arange(n_tiles, dtype=jnp.int32) * tm
    te = jnp.sum((tile_start[:, None] >= ends[None, :]).astype(jnp.int32), axis=1)
    te = jnp.where(jnp.arange(n_tiles) < n_used, te, te[n_used - 1])
    return src, pos.astype(jnp.int32), te.astype(jnp.int32), n_used.reshape(1)


def _moe_kernel(te_ref, nu_ref, x_ref, wg_ref, wu_ref, wd_ref, y_ref, acc):
    i = pl.program_id(0)
    f = pl.program_id(1)
    last = f == pl.num_programs(1) - 1
    used = i < nu_ref[0]

    @pl.when(used)
    def _():
        x = x_ref[...].astype(BF16)
        gate = jnp.dot(x, wg_ref[0].astype(BF16), preferred_element_type=F32)
        up = jnp.dot(x, wu_ref[0].astype(BF16), preferred_element_type=F32)
        part = jnp.dot((_silu(gate) * up).astype(BF16), wd_ref[0].astype(BF16),
                       preferred_element_type=F32)

        @pl.when(f == 0)
        def _():
            acc[...] = part

        @pl.when(f > 0)
        def _():
            acc[...] += part

        @pl.when(last)
        def _():
            y_ref[...] = acc[...]

    @pl.when(jnp.logical_not(used) & last)
    def _():
        y_ref[...] = jnp.zeros(y_ref.shape, y_ref.dtype)


def moe_grouped(xs, te, n_used, wg, wu, wd, tf):
    n_rows = xs.shape[0]
    E, _, F = wg.shape
    nf = F // tf

    def xrow(i, f, te, nu):
        return (jnp.minimum(i, nu[0] - 1), 0)

    def fcol(i, f, nu):
        return jnp.where(i < nu[0], f, nf - 1)

    return pl.pallas_call(
        _moe_kernel,
        grid_spec=pltpu.PrefetchScalarGridSpec(
            num_scalar_prefetch=2,
            grid=(n_rows // MOE_TM, nf),
            in_specs=[pl.BlockSpec((MOE_TM, D_MODEL), xrow),
                      pl.BlockSpec((1, D_MODEL, tf), lambda i, f, te, nu: (te[i], 0, fcol(i, f, nu))),
                      pl.BlockSpec((1, D_MODEL, tf), lambda i, f, te, nu: (te[i], 0, fcol(i, f, nu))),
                      pl.BlockSpec((1, tf, D_MODEL), lambda i, f, te, nu: (te[i], fcol(i, f, nu), 0))],
            out_specs=pl.BlockSpec((MOE_TM, D_MODEL), lambda i, f, te, nu: (i, 0)),
            scratch_shapes=[pltpu.VMEM((MOE_TM, D_MODEL), F32)]),
        out_shape=jax.ShapeDtypeStruct((n_rows, D_MODEL), F32),
        compiler_params=_params("arbitrary", "arbitrary"),
        name="moe_grouped",
    )(te, n_used, xs, wg, wu, wd)


def _combine_kernel(y0_ref, y1_ref, rt_ref, x_ref, mod_ref, gn_ref, x2_ref):
    f = rt_ref[:, 2:3] * y0_ref[...] + rt_ref[:, 3:4] * y1_ref[...]
    x2_ref[...] = x_ref[...] + mod_ref[0, 5:6, :] * _rms(f, gn_ref[3:4, :])


def moe_combine(yg, route, x1, modt, gn):
    n_tok = x1.shape[0]
    n = n_tok // ROW_TILE
    row = lambda i: (i, 0)
    return pl.pallas_call(
        _combine_kernel,
        grid=(n,),
        in_specs=[pl.BlockSpec((ROW_TILE, D_MODEL), row),
                  pl.BlockSpec((ROW_TILE, D_MODEL), lambda i: (n + i, 0)),
                  pl.BlockSpec((ROW_TILE, LANES), row),
                  pl.BlockSpec((ROW_TILE, D_MODEL), row),
                  pl.BlockSpec((1, 6, D_MODEL), lambda i: (i, 0, 0)),
                  pl.BlockSpec((4, D_MODEL), lambda i: (0, 0))],
        out_specs=pl.BlockSpec((ROW_TILE, D_MODEL), row),
        out_shape=jax.ShapeDtypeStruct((n_tok, D_MODEL), F32),
        compiler_params=_params("parallel"),
        name="moe_combine",
    )(yg, yg, route, x1, modt, gn)


def _pack_w_in(w):
    d = w.shape[0]
    qw = N_Q_HEADS * HEAD_DIM
    kw = N_KV_HEADS * HEAD_DIM
    dw = DN_HEADS * DN_D
    o_k = qw
    o_v = o_k + kw
    o_qkv = o_v + kw
    o_z = o_qkv + 3 * dw
    o_ba = o_z + dw
    o_ga = o_ba + 4 * DN_HEADS
    o_gb = o_ga + D_MODEL
    pad = jnp.zeros((d, N_PROJ - w.shape[1]), w.dtype)
    return jnp.concatenate([w[:, :qw], w[:, o_ga:o_gb], w[:, o_gb:], w[:, o_z:o_ba], w[:, o_qkv:o_z],
                            w[:, o_k:o_v], w[:, o_v:o_qkv], w[:, o_ba:o_ga], pad], axis=1).astype(BF16)


def _lane_consts(p):
    out = jnp.zeros((DN_HEADS, 1, LANES), F32)
    out = out.at[:, 0, 2].set(p[0])
    return out.at[:, 0, 3].set(p[1])


def kernel(x_prompt, x_sample, c, cache_k_l0, cache_v_l0, state_delta_l0, cache_k_l1, cache_v_l1,
           state_delta_l1, c_ctx, w_ada, b_ada, norm_g, w_in, attn_sink, conv_w, dn_a_log, dn_dt_bias,
           dn_norm_g, w_pa, w_pb, w_o, w_ff_gate, w_ff_up, w_ff_down, w_router, w_e_gate, w_e_up,
           w_e_down):
    Bc, Tc, _ = x_prompt.shape
    Bl, Tl, _ = x_sample.shape
    n_ctx = Bc * Tc
    n_tok = n_ctx + Bl * Tl
    depth = w_in.shape[0]
    caches = ((cache_k_l0, cache_v_l0, state_delta_l0), (cache_k_l1, cache_v_l1, state_delta_l1))

    x = jnp.concatenate([x_prompt.reshape(n_ctx, D_MODEL), x_sample.reshape(Bl * Tl, D_MODEL)], axis=0)
    cond = jnp.zeros((SUBLANES, D_MODEL), F32).at[:Bl].set(c).at[Bl].set(c_ctx)
    mod = ada_mod(cond, w_ada, b_ada)
    tile_row = np.concatenate([np.full(n_ctx // ROW_TILE, Bl),
                               np.repeat(np.arange(Bl), Tl // ROW_TILE)]).astype(np.int32)
    modt = mod[:, tile_row].reshape(depth, n_tok // ROW_TILE, 6, D_MODEL)
    cos, sin = rope_tables(Tl)

    outs = []
    h = norm_mod(x, norm_g[0, 0:1], modt[0])
    for l in range(depth):
        proj = matmul(h, _pack_w_in(w_in[l]), 512, 1024)
        k_c, v_c, s_c = caches[l]
        oa_c = ctx_attention(proj, attn_sink[l], Bc, Tc)
        oa_l = lat_attention(proj, attn_sink[l], k_c, v_c, cos, sin, n_ctx, Bl, Tl)
        alog = _lane_consts(dn_a_log[l])
        dtb = _lane_consts(dn_dt_bias[l])
        gdn = dn_norm_g[l].reshape(1, DN_D)
        ob_c, s_fin = deltanet(proj, conv_w[l], alog, dtb, gdn, None, 0, Bc, Tc,
                               max(1, DN_GROUP // (2 * (Tc // DN_CHUNK))))
        ob_l = deltanet(proj, conv_w[l], alog, dtb, gdn, s_c, n_ctx, Bl, Tl,
                        max(1, DN_GROUP // (2 * (Tl // DN_CHUNK))))
        moe = l % 2 == 1
        i = l // 2
        wr = None
        if moe:
            wr = jnp.zeros((D_MODEL, LANES), F32).at[:, :N_EXPERTS].set(w_router[i])
        res = mix(oa_c, oa_l, ob_c, ob_l, proj, x, modt[l], norm_g[l], w_pa[l].astype(BF16),
                  w_pb[l].astype(BF16), w_o[l].astype(BF16), wr)
        if moe:
            x1, h2, route = res
            src, pos, te, n_used = route_plan(route, MOE_TM)
            xs = gather_rows(h2, src)
            ys = moe_grouped(xs, te, n_used, w_e_gate[i], w_e_up[i], w_e_down[i], 512)
            yg = gather_rows(ys, pos)
            x = moe_combine(yg, route, x1, modt[l], norm_g[l])
            h = None if l + 1 == depth else norm_mod(x, norm_g[l + 1, 0:1], modt[l + 1])
        else:
            x1, h2 = res
            nxt = l + 1 < depth
            r = ffn_dense(h2, w_ff_gate[i].astype(BF16), w_ff_up[i].astype(BF16),
                          w_ff_down[i].astype(BF16), x1, modt[l], norm_g[l],
                          modt[l + 1] if nxt else None, norm_g[l + 1] if nxt else None, 1408)
            x, h = (r[0], r[1]) if nxt else (r[0], None)
        k_out = proj[:n_ctx, C_K:C_K + N_KV_HEADS * HEAD_DIM].reshape(Bc, Tc, N_KV_HEADS, HEAD_DIM)
        v_out = proj[:n_ctx, C_V:C_V + N_KV_HEADS * HEAD_DIM].reshape(Bc, Tc, N_KV_HEADS, HEAD_DIM)
        outs.append((k_out, v_out, s_fin))

    y_prompt = x[:n_ctx].reshape(Bc, Tc, D_MODEL)
    y_sample = x[n_ctx:].reshape(Bl, Tl, D_MODEL)
    return (y_prompt, y_sample, outs[0][0], outs[0][1], outs[0][2], outs[1][0], outs[1][1], outs[1][2])
```

```python
import functools

import numpy as np
import jax
import jax.numpy as jnp
from jax import lax
from jax.experimental import pallas as pl
from jax.experimental.pallas import tpu as pltpu
from jax.experimental.pallas import tpu_sc as plsc

F32 = jnp.float32
BF16 = jnp.bfloat16
HIGHEST = lax.Precision.HIGHEST

D_MODEL = 1024
HEAD_DIM = 128
N_Q_HEADS = 8
N_KV_HEADS = 2
GQA = N_Q_HEADS // N_KV_HEADS
WINDOW = 128
GRID_W = 64
ROPE_BASE = 10000.0
DN_HEADS = 8
DN_D = 128
DN_CONV = 5
DN_CHUNK = 256
DN_LEVELS = (16, 64)
DN_GROUP = 4
DN_HPS = 2
N_EXPERTS = 8
TOP_K = 2
EPS = 1e-6
NEG_INF = -1e30

LANES = 128
SUBLANES = 8

ROW_TILE = 256
PROJ_TILE = 1024
FFN_TM = 512
MOE_TM = 512
SC_WINDOW = 32
SC_CORES = 2
SC_SUBCORES = 16

C_Q = 0
C_GA = 1024
C_GB = 2048
C_Z = 3072
C_QB = 4096
C_KB = 5120
C_VB = 6144
N_MAIN = 7168
C_K = 0
C_V = 256
C_BA = 512
N_SIDE = 1024

VMEM_LIMIT = 56 * 1024 * 1024


def _params(*sem):
    return pltpu.CompilerParams(dimension_semantics=sem, vmem_limit_bytes=VMEM_LIMIT)


def _rms(x, g):
    return x * lax.rsqrt(jnp.mean(x * x, axis=-1, keepdims=True) + EPS) * g


def _silu(x):
    return x * jax.nn.sigmoid(x)


def _bdot(a, b):
    return jnp.dot(a.astype(BF16), b.astype(BF16), preferred_element_type=F32)


def _bdot_nt(a, b):
    return lax.dot_general(a.astype(BF16), b.astype(BF16), (((1,), (1,)), ((), ())),
                           preferred_element_type=F32)


def _fdot(a, b):
    return jnp.dot(a, b, precision=HIGHEST, preferred_element_type=F32)


def _pieces(x):
    p1 = x.astype(BF16)
    r1 = x - p1.astype(F32)
    p2 = r1.astype(BF16)
    return p1, p2, (r1 - p2.astype(F32)).astype(BF16)


def _dot_split_rhs(a, b):
    dot = functools.partial(jnp.dot, preferred_element_type=F32)
    b1, b2, b3 = _pieces(b)
    return dot(a, b1) + dot(a, b2) + dot(a, b3)


def _dot_split_both(a, b):
    dot = functools.partial(jnp.dot, preferred_element_type=F32)
    ah = a.astype(BF16)
    al = (a - ah.astype(F32)).astype(BF16)
    bh = b.astype(BF16)
    bl = (b - bh.astype(F32)).astype(BF16)
    return dot(ah, bh) + (dot(ah, bl) + dot(al, bh))


def _dot_split_lhs(a, b):
    dot = functools.partial(jnp.dot, preferred_element_type=F32)
    a1, a2, a3 = _pieces(a)
    return dot(a1, b) + dot(a2, b) + dot(a3, b)


def _ada_kernel(c_ref, w_ref, b_ref, o_ref):
    o_ref[0] = _fdot(_silu(c_ref[...]), w_ref[0]) + b_ref[0]


def ada_mod(cond, w_ada, b_ada):
    L = w_ada.shape[0]
    n = w_ada.shape[2] // D_MODEL
    return pl.pallas_call(
        _ada_kernel,
        grid=(L, n),
        in_specs=[pl.BlockSpec((SUBLANES, D_MODEL), lambda l, j: (0, 0)),
                  pl.BlockSpec((1, D_MODEL, D_MODEL), lambda l, j: (l, 0, j)),
                  pl.BlockSpec((1, 1, D_MODEL), lambda l, j: (l, 0, j))],
        out_specs=pl.BlockSpec((1, SUBLANES, D_MODEL), lambda l, j: (l, 0, j)),
        out_shape=jax.ShapeDtypeStruct((L, SUBLANES, 6 * D_MODEL), F32),
        compiler_params=_params("parallel", "parallel"),
        name="ada_mod",
    )(cond, w_ada, b_ada.reshape(L, 1, 6 * D_MODEL))


def _norm_mod_kernel(x_ref, g_ref, mod_ref, h_ref):
    h = _rms(x_ref[...], g_ref[...]) * (1.0 + mod_ref[0, 1:2, :]) + mod_ref[0, 0:1, :]
    h_ref[...] = h.astype(h_ref.dtype)


def norm_mod(x, g, modt):
    n = x.shape[0] // ROW_TILE
    return pl.pallas_call(
        _norm_mod_kernel,
        grid=(n,),
        in_specs=[pl.BlockSpec((ROW_TILE, D_MODEL), lambda i: (i, 0)),
                  pl.BlockSpec((1, D_MODEL), lambda i: (0, 0)),
                  pl.BlockSpec((1, 6, D_MODEL), lambda i: (i, 0, 0))],
        out_specs=pl.BlockSpec((ROW_TILE, D_MODEL), lambda i: (i, 0)),
        out_shape=jax.ShapeDtypeStruct(x.shape, BF16),
        compiler_params=_params("parallel"),
        name="norm_mod",
    )(x, g, modt)


def _mm_kernel(a_ref, b_ref, o_ref):
    o_ref[...] = jnp.dot(a_ref[...], b_ref[...], preferred_element_type=F32).astype(o_ref.dtype)


def matmul(a, b, out_dtype):
    M, K = a.shape
    N = b.shape[1]
    tm = tn = PROJ_TILE
    return pl.pallas_call(
        _mm_kernel,
        grid=(N // tn, M // tm),
        in_specs=[pl.BlockSpec((tm, K), lambda j, i: (i, 0)),
                  pl.BlockSpec((K, tn), lambda j, i: (0, j))],
        out_specs=pl.BlockSpec((tm, tn), lambda j, i: (i, j)),
        out_shape=jax.ShapeDtypeStruct((M, N), out_dtype),
        compiler_params=_params("parallel", "parallel"),
        name="in_proj",
    )(a, b)


def _ctx_attn_kernel(sink_ref, q_ref, k_ref, v_ref, o_ref):
    hk = pl.program_id(1)
    k = k_ref[...].astype(BF16)
    v = v_ref[...].astype(BF16)
    scale = HEAD_DIM ** -0.5
    for g in range(GQA):
        q = q_ref[:, g * HEAD_DIM:(g + 1) * HEAD_DIM]
        s = _bdot_nt(q, k) * scale
        sink = sink_ref[hk * GQA + g]
        m = jnp.maximum(jnp.max(s, axis=-1, keepdims=True), sink)
        p = jnp.exp(s - m)
        den = jnp.sum(p, axis=-1, keepdims=True) + jnp.exp(sink - m)
        o = _bdot(p, v) / den
        o_ref[:, g * HEAD_DIM:(g + 1) * HEAD_DIM] = o.astype(o_ref.dtype)


def ctx_attention(proj, side, sink, n_seq, T):
    qw = GQA * HEAD_DIM
    return pl.pallas_call(
        _ctx_attn_kernel,
        grid_spec=pltpu.PrefetchScalarGridSpec(
            num_scalar_prefetch=1,
            grid=(n_seq, N_KV_HEADS),
            in_specs=[pl.BlockSpec((T, qw), lambda b, h, s: (b, C_Q // qw + h)),
                      pl.BlockSpec((T, HEAD_DIM), lambda b, h, s: (b, C_K // HEAD_DIM + h)),
                      pl.BlockSpec((T, HEAD_DIM), lambda b, h, s: (b, C_V // HEAD_DIM + h))],
            out_specs=pl.BlockSpec((T, qw), lambda b, h, s: (b, h))),
        out_shape=jax.ShapeDtypeStruct((n_seq * T, N_Q_HEADS * HEAD_DIM), BF16),
        compiler_params=_params("parallel", "parallel"),
        name="ctx_attn",
    )(sink, proj, side, side)


def _rope(x, cos, sin):
    lane = lax.broadcasted_iota(jnp.int32, x.shape, 1)
    q = HEAD_DIM // 4
    swapped = jnp.where((lane % (2 * q)) < q,
                        pltpu.roll(x, HEAD_DIM - q, 1), pltpu.roll(x, q, 1))
    return x * cos + swapped * sin


def _lat_attn_kernel(T, sink_ref, q_ref, k_ref, v_ref, kc_ref, vc_ref, cosq_ref, sinq_ref,
                     cos_ref, sin_ref, o_ref, kpad, vpad):
    hk = pl.program_id(1)
    i = pl.program_id(2)
    B = WINDOW

    @pl.when(i == 0)
    def _():
        zeros = jnp.zeros((B, HEAD_DIM), BF16)
        kpad[0:B, :] = zeros
        kpad[T + B:T + 2 * B, :] = zeros
        vpad[0:B, :] = zeros
        vpad[T + B:T + 2 * B, :] = zeros
        kpad[B:T + B, :] = _rope(k_ref[...], cos_ref[...], sin_ref[...]).astype(BF16)
        vpad[B:T + B, :] = v_ref[...].astype(BF16)

    start = pl.multiple_of(i * B, B)
    kw = kpad[pl.ds(start, 3 * B), :]
    vw = vpad[pl.ds(start, 3 * B), :]
    kc = kc_ref[0].astype(BF16)
    vc = vc_ref[0].astype(BF16)

    rows = GQA * B
    qi = lax.broadcasted_iota(jnp.int32, (rows, 3 * B), 0) % B
    kj = lax.broadcasted_iota(jnp.int32, (rows, 3 * B), 1) - B
    kpos = kj + i * B
    mask = (jnp.abs(qi - kj) <= WINDOW) & (kpos >= 0) & (kpos < T)

    cq = cosq_ref[...]
    sq = sinq_ref[...]
    q = jnp.concatenate(
        [_rope(q_ref[:, g * HEAD_DIM:(g + 1) * HEAD_DIM].astype(F32), cq, sq) for g in range(GQA)],
        axis=0)
    scale = HEAD_DIM ** -0.5
    s_c = _bdot_nt(q, kc) * scale
    s_w = jnp.where(mask, _bdot_nt(q, kw) * scale, NEG_INF)
    gi = lax.broadcasted_iota(jnp.int32, (rows, 1), 0) // B
    sink = jnp.zeros((rows, 1), F32)
    for g in range(GQA):
        sink = jnp.where(gi == g, sink_ref[hk * GQA + g], sink)
    m = jnp.maximum(jnp.maximum(jnp.max(s_c, axis=-1, keepdims=True),
                                jnp.max(s_w, axis=-1, keepdims=True)), sink)
    p_c = jnp.exp(s_c - m)
    p_w = jnp.exp(s_w - m)
    den = (jnp.sum(p_c, axis=-1, keepdims=True) + jnp.sum(p_w, axis=-1, keepdims=True)
           + jnp.exp(sink - m))
    o = (_bdot(p_c, vc) + _bdot(p_w, vw)) / den
    for g in range(GQA):
        o_ref[:, g * HEAD_DIM:(g + 1) * HEAD_DIM] = o[g * B:(g + 1) * B, :].astype(o_ref.dtype)


def lat_attention(proj, side, sink, cache_k, cache_v, cos, sin, row0, n_seq, T):
    qw = GQA * HEAD_DIM
    B = WINDOW
    nb = T // B
    P = cache_k.shape[1]
    kc = cache_k.reshape(n_seq, P, N_KV_HEADS * HEAD_DIM)
    vc = cache_v.reshape(n_seq, P, N_KV_HEADS * HEAD_DIM)
    rb = row0 // B
    rt = row0 // T
    return pl.pallas_call(
        functools.partial(_lat_attn_kernel, T),
        grid_spec=pltpu.PrefetchScalarGridSpec(
            num_scalar_prefetch=1,
            grid=(n_seq, N_KV_HEADS, nb),
            in_specs=[pl.BlockSpec((B, qw), lambda b, h, i, s: (rb + b * nb + i, C_Q // qw + h)),
                      pl.BlockSpec((T, HEAD_DIM), lambda b, h, i, s: (rt + b, C_K // HEAD_DIM + h)),
                      pl.BlockSpec((T, HEAD_DIM), lambda b, h, i, s: (rt + b, C_V // HEAD_DIM + h)),
                      pl.BlockSpec((1, P, HEAD_DIM), lambda b, h, i, s: (b, 0, h)),
                      pl.BlockSpec((1, P, HEAD_DIM), lambda b, h, i, s: (b, 0, h)),
                      pl.BlockSpec((B, HEAD_DIM), lambda b, h, i, s: (i, 0)),
                      pl.BlockSpec((B, HEAD_DIM), lambda b, h, i, s: (i, 0)),
                      pl.BlockSpec((T, HEAD_DIM), lambda b, h, i, s: (0, 0)),
                      pl.BlockSpec((T, HEAD_DIM), lambda b, h, i, s: (0, 0))],
            out_specs=pl.BlockSpec((B, qw), lambda b, h, i, s: (b * nb + i, h)),
            scratch_shapes=[pltpu.VMEM((T + 2 * B, HEAD_DIM), BF16),
                            pltpu.VMEM((T + 2 * B, HEAD_DIM), BF16)]),
        out_shape=jax.ShapeDtypeStruct((n_seq * T, N_Q_HEADS * HEAD_DIM), BF16),
        compiler_params=_params("parallel", "parallel", "arbitrary"),
        name="lat_attn",
    )(sink, proj, side, side, kc, vc, cos, sin, cos, sin)


def rope_tables(T):
    rows = (np.arange(T) // GRID_W).astype(np.float32)
    cols = (np.arange(T) % GRID_W).astype(np.float32)
    n_freq = HEAD_DIM // 4
    inv = jnp.asarray(ROPE_BASE, F32) ** (-jnp.arange(n_freq, dtype=F32) / n_freq)
    ar = jnp.asarray(rows)[:, None] * inv
    ac = jnp.asarray(cols)[:, None] * inv
    cos = jnp.concatenate([jnp.cos(ar), jnp.cos(ar), jnp.cos(ac), jnp.cos(ac)], axis=-1)
    sin = jnp.concatenate([-jnp.sin(ar), jnp.sin(ar), -jnp.sin(ac), jnp.sin(ac)], axis=-1)
    return cos, sin


def _dn_kernel(T, hps, has_s0, *refs):
    n_in = 12 if has_s0 else 11
    (q_ref, k_ref, v_ref, z_ref, ba_ref, wq_ref, wk_ref, wv_ref, alog_ref, dt_ref, gn_ref) = refs[:11]
    s0_ref = refs[11] if has_s0 else None
    if has_s0:
        o_ref = refs[n_in]
        sfin_ref = None
        scratch = refs[n_in + 1:]
    else:
        o_ref, sfin_ref = refs[n_in], refs[n_in + 1]
        scratch = refs[n_in + 2:]
    xpad, qs, ks, vs, cols, gcs, gcT, kTs, U, WQ, QKM, KDT, EGL, OST = scratch

    hg = pl.program_id(1)
    C = DN_CHUNK
    D = DN_D
    n = T // C
    half = DN_CONV // 2
    PAD = SUBLANES

    zpad = jnp.zeros((PAD, D), F32)
    xpad[0:PAD, :] = zpad
    xpad[T + PAD:T + 2 * PAD, :] = zpad

    def conv_silu(x_ref, w_ref, hh):
        xpad[PAD:T + PAD, :] = x_ref[:, hh * D:(hh + 1) * D].astype(F32)
        w = w_ref[:, hh * D:(hh + 1) * D]
        acc = xpad[pl.ds(PAD - half, T), :] * w[0:1, :]
        for j in range(1, DN_CONV):
            acc = acc + xpad[pl.ds(PAD - half + j, T), :] * w[j:j + 1, :]
        return _silu(acc)

    def l2n(x):
        return x * lax.rsqrt(jnp.sum(x * x, axis=-1, keepdims=True) + EPS)

    ri = lax.broadcasted_iota(jnp.int32, (C, C), 0)
    ci = lax.broadcasted_iota(jnp.int32, (C, C), 1)
    eye = jnp.where(ri == ci, 1.0, 0.0).astype(F32)
    same_blk = {s: (ri // s) == (ci // s) for s in DN_LEVELS}
    lower =jnp.where(ri >= ci, 1.0, 0.0).astype(BF16)
    li = lax.broadcasted_iota(jnp.int32, (LANES, LANES), 0)
    lj = lax.broadcasted_iota(jnp.int32, (LANES, LANES), 1)
    lane_t = lax.broadcasted_iota(jnp.int32, (T, LANES), 1)
    lane_c = lax.broadcasted_iota(jnp.int32, (C, LANES), 1)

    for hh in range(hps):
        qs[hh] = l2n(conv_silu(q_ref, wq_ref, hh)) * (D ** -0.5)
        ks[hh] = l2n(conv_silu(k_ref, wk_ref, hh))
        vs[hh] = conv_silu(v_ref, wv_ref, hh)
        head = hg * hps + hh
        sel = jnp.where((lj < 4) & (li == lj * DN_HEADS + head), 1.0, 0.0).astype(BF16)
        raw = _dot_split_lhs(ba_ref[...], sel)
        gate = -jnp.exp(alog_ref[hh]) * jax.nn.softplus(raw + dt_ref[hh])
        cols[hh] = jnp.where(lane_t < 2, jax.nn.sigmoid(raw), gate)
        for c in range(n):
            rows = slice(c * C, (c + 1) * C)
            g = cols[hh, rows, :]
            pre = _dot_split_rhs(lower, g)
            gc = jnp.where(lane_c == 3, pre[C - 1:C, :] - pre + g, pre)
            gcs[hh, rows, :] = gc
            gcT[hh * n + c] = gc.T[0:SUBLANES, :]
            kTs[hh * n + c] = ks[hh, rows, :].T

    chains = [(hh, c, d) for hh in range(hps) for c in range(n) for d in (0, 1)]
    chain_id = {ch: j for j, ch in enumerate(chains)}

    def nilpotent_inverse(ms, squarings):
        ts = [eye + m for m in ms]
        pws = ms
        for _ in range(squarings):
            pws = [_bdot(pw, pw) for pw in pws]
            ts = [t + _bdot(t, pw) for t, pw in zip(ts, pws)]
        return ts

    def phase_a(j0, grp):
        G = range(len(grp))
        rows = [slice(c * C, (c + 1) * C) for (_, c, _) in grp]
        incl = [(ri >= ci) if d == 0 else (ri <= ci) for (_, _, d) in grp]
        strict = [(ri > ci) if d == 0 else (ri < ci) for (_, _, d) in grp]
        q = [qs[grp[i][0], rows[i], :] for i in G]
        k = [ks[grp[i][0], rows[i], :] for i in G]
        v = [vs[grp[i][0], rows[i], :] for i in G]
        gcc = [gcs[hh, rows[i], 2 + d:3 + d] for i, (hh, _, d) in enumerate(grp)]
        bcol = [cols[hh, rows[i], d:d + 1] for i, (hh, _, d) in enumerate(grp)]
        kT = [kTs[hh * n + c] for (hh, c, _) in grp]
        gcr = [gcT[hh * n + c, 2 + d:3 + d, :] for (hh, c, d) in grp]
        g_last = [gcr[i][:, C - 1:C] if grp[i][2] == 0 else gcr[i][:, 0:1] for i in G]
        decay = [jnp.where(incl[i], jnp.exp(jnp.where(incl[i], gcc[i] - gcr[i], 0.0)), 0.0) for i in G]
        qkk_of = {}
        for i in G:
            if grp[i][:2] not in qkk_of:
                qkk_of[grp[i][:2]] = _bdot(jnp.concatenate([q[i], k[i]], axis=0), kT[i])
        qkk = [qkk_of[grp[i][:2]] for i in G]
        a = [jnp.where(strict[i], qkk[i][C:2 * C, :] * bcol[i] * decay[i], 0.0) for i in G]
        sizes = DN_LEVELS + (C,)
        t_inv = nilpotent_inverse([jnp.where(same_blk[sizes[0]], -a[i], 0.0) for i in G],
                                  sizes[0].bit_length() - 2)
        for prev, size in zip(sizes[:-1], sizes[1:]):
            off = [jnp.where(same_blk[prev], 0.0, -a[i]) for i in G]
            if size < C:
                off = [jnp.where(same_blk[size], off[i], 0.0) for i in G]
            e_neg = [_bdot(t_inv[i], off[i]) for i in G]
            f = nilpotent_inverse(e_neg, (size // prev).bit_length() - 2)
            t_inv = [_bdot(f[i], t_inv[i]) for i in G]
        egc = [jnp.exp(gcc[i]) for i in G]
        uw = [_bdot(t_inv[i], jnp.concatenate([v[i] * bcol[i], k[i] * (bcol[i] * egc[i])], axis=1))
              for i in G]
        r0, r1 = j0, j0 + len(grp)
        U[r0 * C:r1 * C, :] = jnp.concatenate([uw[i][:, 0:D] for i in G], axis=0)
        WQ[r0 * 2 * C:r1 * 2 * C, :] = jnp.concatenate(
            [jnp.concatenate([uw[i][:, D:2 * D], q[i] * egc[i]], axis=0) for i in G], axis=0).astype(BF16)
        QKM[r0 * C:r1 * C, :] = jnp.concatenate(
            [qkk[i][0:C, :] * decay[i] for i in G], axis=0).astype(BF16)
        KDT[r0 * D:r1 * D, :] = jnp.concatenate(
            [kT[i] * jnp.exp(g_last[i] - gcr[i]) for i in G], axis=0).astype(BF16)
        EGL[r0 * SUBLANES:r1 * SUBLANES, :] = jnp.concatenate(
            [jnp.broadcast_to(jnp.exp(g_last[i]), (SUBLANES, LANES)) for i in G], axis=0)

    for j0 in range(0, len(chains), DN_GROUP):
        phase_a(j0, chains[j0:j0 + DN_GROUP])

    if has_s0:
        state = [s0_ref[0, d, hh] for hh in range(hps) for d in (0, 1)]
    else:
        state = [jnp.zeros((D, D), F32) for _ in range(2 * hps)]
    for i in range(n):
        outs = []
        new_state = []
        for hh in range(hps):
            for d in (0, 1):
                j = chain_id[(hh, i if d == 0 else n - 1 - i, d)]
                s_old = state[hh * 2 + d]
                xs = jnp.dot(WQ[j * 2 * C:(j + 1) * 2 * C, :], s_old.astype(BF16),
                             preferred_element_type=F32)
                v_new = (U[j * C:(j + 1) * C, :] - xs[0:C, :]).astype(BF16)
                outs.append(xs[C:2 * C, :] + jnp.dot(QKM[j * C:(j + 1) * C, :], v_new,
                                                     preferred_element_type=F32))
                new_state.append(s_old * EGL[j * SUBLANES:j * SUBLANES + 1, 0:1]
                                 + jnp.dot(KDT[j * D:(j + 1) * D, :], v_new, preferred_element_type=F32))
        OST[i] = jnp.concatenate(outs, axis=0)
        state = new_state

    if not has_s0:
        for hh in range(hps):
            for d in (0, 1):
                sfin_ref[0, d, hh] = state[hh * 2 + d]

    for hh in range(hps):
        for c in range(n):
            rows = slice(c * C, (c + 1) * C)
            o = (OST[c, (2 * hh) * C:(2 * hh + 1) * C, :]
                 + OST[n - 1 - c, (2 * hh + 1) * C:(2 * hh + 2) * C, :])
            y = _rms(o, gn_ref[...]) * _silu(z_ref[rows, hh * D:(hh + 1) * D].astype(F32))
            o_ref[rows, hh * D:(hh + 1) * D] = y.astype(o_ref.dtype)


def deltanet(proj, side, conv_w, alog_l, dt_l, gn, s0, row0, n_seq, T, hps):
    rt = row0 // T
    has_s0 = s0 is not None
    W = hps * DN_D
    n = T // DN_CHUNK
    n_chain = hps * n * 2

    def col(c0):
        return pl.BlockSpec((T, W), lambda b, h: (rt + b, c0 // W + h))

    def wcol(c0):
        return pl.BlockSpec((DN_CONV, W), lambda b, h: (0, c0 // W + h))

    in_specs = [col(C_QB), col(C_KB), col(C_VB), col(C_Z),
                pl.BlockSpec((T, LANES), lambda b, h: (rt + b, C_BA // LANES)),
                wcol(0), wcol(DN_HEADS * DN_D), wcol(2 * DN_HEADS * DN_D),
                pl.BlockSpec((hps, 1, LANES), lambda b, h: (h, 0, 0)),
                pl.BlockSpec((hps, 1, LANES), lambda b, h: (h, 0, 0)),
                pl.BlockSpec((1, DN_D), lambda b, h: (0, 0))]
    args = [proj, proj, proj, proj, side, conv_w, conv_w, conv_w, alog_l, dt_l, gn]
    o_spec = pl.BlockSpec((T, W), lambda b, h: (b, h))
    o_shape = jax.ShapeDtypeStruct((n_seq * T, DN_HEADS * DN_D), BF16)
    s_spec = pl.BlockSpec((1, 2, hps, DN_D, DN_D), lambda b, h: (b, 0, h, 0, 0))
    if has_s0:
        in_specs.append(s_spec)
        args.append(s0)
        out_specs, out_shape = o_spec, o_shape
    else:
        out_specs = (o_spec, s_spec)
        out_shape = (o_shape, jax.ShapeDtypeStruct((n_seq, 2, DN_HEADS, DN_D, DN_D), F32))
    C = DN_CHUNK
    scratch = [pltpu.VMEM((T + 2 * SUBLANES, DN_D), F32),
               pltpu.VMEM((hps, T, DN_D), F32),
               pltpu.VMEM((hps, T, DN_D), F32),
               pltpu.VMEM((hps, T, DN_D), F32),
               pltpu.VMEM((hps, T, LANES), F32),
               pltpu.VMEM((hps, T, LANES), F32),
               pltpu.VMEM((hps * n, SUBLANES, C), F32),
               pltpu.VMEM((hps * n, DN_D, C), F32),
               pltpu.VMEM((n_chain * C, DN_D), F32),
               pltpu.VMEM((n_chain * 2 * C, DN_D), BF16),
               pltpu.VMEM((n_chain * C, C), BF16),
               pltpu.VMEM((n_chain * DN_D, C), BF16),
               pltpu.VMEM((n_chain * SUBLANES, LANES), F32),
               pltpu.VMEM((n, 2 * hps * C, DN_D), F32)]
    return pl.pallas_call(
        functools.partial(_dn_kernel, T, hps, has_s0),
        grid=(n_seq, DN_HEADS // hps),
        in_specs=in_specs,
        out_specs=out_specs,
        out_shape=out_shape,
        scratch_shapes=scratch,
        compiler_params=_params("parallel", "parallel"),
        name="deltanet_lat" if has_s0 else "deltanet_ctx",
    )(*args)


def _route_top2(logits):
    lane = lax.broadcasted_iota(jnp.int32, logits.shape, 1).astype(F32)
    lg = jnp.where(lane < N_EXPERTS, logits, -jnp.inf)
    m1 = jnp.max(lg, axis=-1, keepdims=True)
    i1 = jnp.min(jnp.where(lg == m1, lane, float(LANES)), axis=-1, keepdims=True)
    lg2 = jnp.where(lane == i1, -jnp.inf, lg)
    m2 = jnp.max(lg2, axis=-1, keepdims=True)
    i2 = jnp.min(jnp.where(lg2 == m2, lane, float(LANES)), axis=-1, keepdims=True)
    e2 = jnp.exp(m2 - m1)
    den = 1.0 + e2
    return jnp.where(lane == 0.0, i1, jnp.where(lane == 1.0, i2,
                     jnp.where(lane == 2.0, 1.0 / den, jnp.where(lane == 3.0, e2 / den, 0.0))))


def _mix_kernel(route, n_ctx_tiles, oac_ref, oal_ref, obc_ref, obl_ref, ga_ref, gb_ref, x_ref,
                mod_ref, gn_ref, wpa_ref, wpb_ref, wo_ref, *rest):
    if route:
        wr_ref, x1_ref, h2_ref, rt_ref = rest
    else:
        x1_ref, h2_ref = rest
    is_ctx = pl.program_id(0) < n_ctx_tiles
    o_a = jnp.where(is_ctx, oac_ref[...], oal_ref[...])
    o_b = jnp.where(is_ctx, obc_ref[...], obl_ref[...])
    br_a = jnp.dot(o_a, wpa_ref[...], preferred_element_type=F32)
    br_b = jnp.dot(o_b, wpb_ref[...], preferred_element_type=F32)
    merged = (jax.nn.sigmoid(ga_ref[...].astype(F32)) * br_a
              + jax.nn.sigmoid(gb_ref[...].astype(F32)) * br_b)
    y = jnp.dot(merged.astype(BF16), wo_ref[...], preferred_element_type=F32)
    x1 = x_ref[...] + mod_ref[0, 2:3, :] * _rms(y, gn_ref[1:2, :])
    x1_ref[...] = x1
    h2 = _rms(x1, gn_ref[2:3, :]) * (1.0 + mod_ref[0, 4:5, :]) + mod_ref[0, 3:4, :]
    h2_ref[...] = h2.astype(h2_ref.dtype)
    if route:
        rt_ref[...] = _route_top2(_dot_split_both(h2, wr_ref[...]))


def mix(oa_c, oa_l, ob_c, ob_l, proj, x, modt, gn, wpa, wpb, wo, w_router):
    n_tok = x.shape[0]
    n = n_tok // ROW_TILE
    nc = oa_c.shape[0] // ROW_TILE
    nl = oa_l.shape[0] // ROW_TILE
    route = w_router is not None
    row = lambda i: (i, 0)
    full = lambda i: (0, 0)
    crow = lambda i: (jnp.minimum(i, nc - 1), 0)
    lrow = lambda i: (jnp.maximum(i - nc, 0), 0)
    in_specs = [pl.BlockSpec((ROW_TILE, D_MODEL), crow),
                pl.BlockSpec((ROW_TILE, D_MODEL), lrow),
                pl.BlockSpec((ROW_TILE, D_MODEL), crow),
                pl.BlockSpec((ROW_TILE, D_MODEL), lrow),
                pl.BlockSpec((ROW_TILE, D_MODEL), lambda i: (i, C_GA // D_MODEL)),
                pl.BlockSpec((ROW_TILE, D_MODEL), lambda i: (i, C_GB // D_MODEL)),
                pl.BlockSpec((ROW_TILE, D_MODEL), row),
                pl.BlockSpec((1, 6, D_MODEL), lambda i: (i, 0, 0)),
                pl.BlockSpec((4, D_MODEL), full),
                pl.BlockSpec((D_MODEL, D_MODEL), full),
                pl.BlockSpec((D_MODEL, D_MODEL), full),
                pl.BlockSpec((D_MODEL, D_MODEL), full)]
    args = [oa_c, oa_l, ob_c, ob_l, proj, proj, x, modt, gn, wpa, wpb, wo]
    out_specs = [pl.BlockSpec((ROW_TILE, D_MODEL), row), pl.BlockSpec((ROW_TILE, D_MODEL), row)]
    out_shape = [jax.ShapeDtypeStruct((n_tok, D_MODEL), F32),
                 jax.ShapeDtypeStruct((n_tok, D_MODEL), F32 if route else BF16)]
    if route:
        in_specs.append(pl.BlockSpec((D_MODEL, LANES), full))
        args.append(w_router)
        out_specs.append(pl.BlockSpec((ROW_TILE, LANES), row))
        out_shape.append(jax.ShapeDtypeStruct((n_tok, LANES), F32))
    assert nc + nl == n
    return pl.pallas_call(
        functools.partial(_mix_kernel, route, nc),
        grid=(n,),
        in_specs=in_specs,
        out_specs=tuple(out_specs),
        out_shape=tuple(out_shape),
        compiler_params=_params("parallel"),
        name="mix_route" if route else "mix",
    )(*args)


def _ffn_kernel(next_norm, h_ref, wg_ref, wu_ref, wd_ref, x_ref, mod_ref, gn_ref, *rest):
    if next_norm:
        modn_ref, gnn_ref, x2_ref, hn_ref, acc = rest
    else:
        x2_ref, acc = rest
    f = pl.program_id(1)
    h = h_ref[...]
    gate = jnp.dot(h, wg_ref[...], preferred_element_type=F32)
    up = jnp.dot(h, wu_ref[...], preferred_element_type=F32)
    part = jnp.dot((_silu(gate) * up).astype(BF16), wd_ref[...], preferred_element_type=F32)
    acc[...] = jnp.where(f == 0, 0.0, acc[...]) + part

    @pl.when(f == pl.num_programs(1) - 1)
    def _():
        for r in range(FFN_TM // ROW_TILE):
            rs = slice(r * ROW_TILE, (r + 1) * ROW_TILE)
            x2 = x_ref[rs, :] + mod_ref[r, 5:6, :] * _rms(acc[rs, :], gn_ref[3:4, :])
            x2_ref[rs, :] = x2
            if next_norm:
                hn = _rms(x2, gnn_ref[0:1, :]) * (1.0 + modn_ref[r, 1:2, :]) + modn_ref[r, 0:1, :]
                hn_ref[rs, :] = hn.astype(hn_ref.dtype)


def ffn_dense(h2, wg, wu, wd, x1, modt, gn, modt_next, gn_next, tf):
    n_tok = x1.shape[0]
    F = wg.shape[1]
    nr = FFN_TM // ROW_TILE
    next_norm = modt_next is not None
    row = lambda i, f: (i, 0)
    in_specs = [pl.BlockSpec((FFN_TM, D_MODEL), row),
                pl.BlockSpec((D_MODEL, tf), lambda i, f: (0, f)),
                pl.BlockSpec((D_MODEL, tf), lambda i, f: (0, f)),
                pl.BlockSpec((tf, D_MODEL), lambda i, f: (f, 0)),
                pl.BlockSpec((FFN_TM, D_MODEL), row),
                pl.BlockSpec((nr, 6, D_MODEL), lambda i, f: (i, 0, 0)),
                pl.BlockSpec((4, D_MODEL), lambda i, f: (0, 0))]
    args = [h2, wg, wu, wd, x1, modt, gn]
    out_specs = [pl.BlockSpec((FFN_TM, D_MODEL), row)]
    out_shape = [jax.ShapeDtypeStruct((n_tok, D_MODEL), F32)]
    if next_norm:
        in_specs += [pl.BlockSpec((nr, 6, D_MODEL), lambda i, f: (i, 0, 0)),
                     pl.BlockSpec((4, D_MODEL), lambda i, f: (0, 0))]
        args += [modt_next, gn_next]
        out_specs.append(pl.BlockSpec((FFN_TM, D_MODEL), row))
        out_shape.append(jax.ShapeDtypeStruct((n_tok, D_MODEL), BF16))
    return pl.pallas_call(
        functools.partial(_ffn_kernel, next_norm),
        grid=(n_tok // FFN_TM, F // tf),
        in_specs=in_specs,
        out_specs=tuple(out_specs),
        out_shape=tuple(out_shape),
        scratch_shapes=[pltpu.VMEM((FFN_TM, D_MODEL), F32)],
        compiler_params=_params("parallel", "arbitrary"),
        name="ffn_dense",
    )(*args)


def gather_rows(x, idx):
    n = idx.shape[0]
    w = x.shape[1]
    per_worker = n // (SC_CORES * SC_SUBCORES)
    steps = per_worker // SC_WINDOW
    assert steps * SC_WINDOW * SC_CORES * SC_SUBCORES == n
    mesh = plsc.VectorSubcoreMesh(core_axis_name="core", subcore_axis_name="subcore")

    @functools.partial(pl.kernel, out_type=jax.ShapeDtypeStruct((n, w), x.dtype), mesh=mesh,
                       scratch_types=[pltpu.VMEM((SC_WINDOW,), jnp.int32),
                                      pltpu.VMEM((SC_WINDOW, w), x.dtype),
                                      pltpu.SemaphoreType.DMA])
    def gather_kernel(x_hbm, i_hbm, o_hbm, idx_v, rows_v, sem):
        worker = lax.axis_index("subcore") * SC_CORES + lax.axis_index("core")
        base = worker * per_worker

        @pl.loop(0, steps)
        def _(s):
            off = base + s * SC_WINDOW
            pltpu.sync_copy(i_hbm.at[pl.ds(off, SC_WINDOW)], idx_v)
            pltpu.async_copy(x_hbm.at[idx_v], rows_v, sem).wait()
            pltpu.sync_copy(rows_v, o_hbm.at[pl.ds(off, SC_WINDOW)])

    return gather_kernel(x, idx)


def dispatch_rows(x, pos, n_rows):
    n_tok, w = x.shape
    per_worker = n_tok // (SC_CORES * SC_SUBCORES)
    steps = per_worker // SC_WINDOW
    assert steps * SC_WINDOW * SC_CORES * SC_SUBCORES == n_tok
    mesh = plsc.VectorSubcoreMesh(core_axis_name="core", subcore_axis_name="subcore")

    @functools.partial(pl.kernel, out_type=jax.ShapeDtypeStruct((n_rows, w), x.dtype), mesh=mesh,
                       scratch_types=[pltpu.VMEM((SC_WINDOW,), jnp.int32),
                                      pltpu.VMEM((SC_WINDOW, w), x.dtype)])
    def dispatch_kernel(x_hbm, p_hbm, o_hbm, idx_v, rows_v):
        worker = lax.axis_index("subcore") * SC_CORES + lax.axis_index("core")
        base = worker * per_worker

        @pl.loop(0, steps)
        def _(s):
            off = base + s * SC_WINDOW
            pltpu.sync_copy(x_hbm.at[pl.ds(off, SC_WINDOW)], rows_v)
            for k in range(TOP_K):
                pltpu.sync_copy(p_hbm.at[pl.ds(k * n_tok + off, SC_WINDOW)], idx_v)
                pltpu.sync_copy(rows_v, o_hbm.at[idx_v])

    return dispatch_kernel(x, pos)


def route_plan(route, tm):
    n_tok = route.shape[0]
    ef = route[:, 0:TOP_K].astype(jnp.int32).T.reshape(-1)
    onehot = (ef[:, None] == jnp.arange(N_EXPERTS, dtype=jnp.int32)[None, :]).astype(jnp.int32)
    csum = jnp.cumsum(onehot, axis=0)
    rank = jnp.sum(csum * onehot, axis=1) - 1
    counts = csum[-1]
    padded = ((counts + tm - 1) // tm) * tm
    ends = jnp.cumsum(padded)
    pos = (ends - padded)[ef] + rank
    n_rows = TOP_K * n_tok + N_EXPERTS * tm
    n_tiles = n_rows // tm
    n_used = (ends[-1] // tm).astype(jnp.int32)
    tile_start = jnp.arange(n_tiles, dtype=jnp.int32) * tm
    te = jnp.sum((tile_start[:, None] >= ends[None, :]).astype(jnp.int32), axis=1)
    te = jnp.where(jnp.arange(n_tiles) < n_used, te, te[n_used - 1])
    return pos.astype(jnp.int32), n_rows, te.astype(jnp.int32), n_used.reshape(1)


def _moe_kernel(te_ref, nu_ref, x_ref, wg_ref, wu_ref, wd_ref, y_ref, acc):
    i = pl.program_id(0)
    f = pl.program_id(1)
    last = f == pl.num_programs(1) - 1
    used = i < nu_ref[0]

    @pl.when(used)
    def _():
        x = x_ref[...].astype(BF16)
        gate = jnp.dot(x, wg_ref[0].astype(BF16), preferred_element_type=F32)
        up = jnp.dot(x, wu_ref[0].astype(BF16), preferred_element_type=F32)
        part = jnp.dot((_silu(gate) * up).astype(BF16), wd_ref[0].astype(BF16),
                       preferred_element_type=F32)
        total = jnp.where(f == 0, 0.0, acc[...]) + part
        acc[...] = total
        y_ref[...] = total

    @pl.when(jnp.logical_not(used) & last)
    def _():
        y_ref[...] = jnp.zeros(y_ref.shape, y_ref.dtype)


def moe_grouped(xs, te, n_used, wg, wu, wd, tf):
    n_rows = xs.shape[0]
    E, _, F = wg.shape
    nf = F // tf

    def xrow(i, f, te, nu):
        return (jnp.minimum(i, nu[0] - 1), 0)

    def fcol(i, f, nu):
        return jnp.where(i < nu[0], f, nf - 1)

    return pl.pallas_call(
        _moe_kernel,
        grid_spec=pltpu.PrefetchScalarGridSpec(
            num_scalar_prefetch=2,
            grid=(n_rows // MOE_TM, nf),
            in_specs=[pl.BlockSpec((MOE_TM, D_MODEL), xrow),
                      pl.BlockSpec((1, D_MODEL, tf), lambda i, f, te, nu: (te[i], 0, fcol(i, f, nu))),
                      pl.BlockSpec((1, D_MODEL, tf), lambda i, f, te, nu: (te[i], 0, fcol(i, f, nu))),
                      pl.BlockSpec((1, tf, D_MODEL), lambda i, f, te, nu: (te[i], fcol(i, f, nu), 0))],
            out_specs=pl.BlockSpec((MOE_TM, D_MODEL), lambda i, f, te, nu: (i, 0)),
            scratch_shapes=[pltpu.VMEM((MOE_TM, D_MODEL), F32)]),
        out_shape=jax.ShapeDtypeStruct((n_rows, D_MODEL), F32),
        compiler_params=_params("arbitrary", "arbitrary"),
        name="moe_grouped",
    )(te, n_used, xs, wg, wu, wd)


def _combine_kernel(y0_ref, y1_ref, rt_ref, x_ref, mod_ref, gn_ref, x2_ref):
    f = rt_ref[:, 2:3] * y0_ref[...] + rt_ref[:, 3:4] * y1_ref[...]
    x2_ref[...] = x_ref[...] + mod_ref[0, 5:6, :] * _rms(f, gn_ref[3:4, :])


def moe_combine(yg, route, x1, modt, gn):
    n_tok = x1.shape[0]
    n = n_tok // ROW_TILE
    row = lambda i: (i, 0)
    return pl.pallas_call(
        _combine_kernel,
        grid=(n,),
        in_specs=[pl.BlockSpec((ROW_TILE, D_MODEL), row),
                  pl.BlockSpec((ROW_TILE, D_MODEL), lambda i: (n + i, 0)),
                  pl.BlockSpec((ROW_TILE, LANES), row),
                  pl.BlockSpec((ROW_TILE, D_MODEL), row),
                  pl.BlockSpec((1, 6, D_MODEL), lambda i: (i, 0, 0)),
                  pl.BlockSpec((4, D_MODEL), lambda i: (0, 0))],
        out_specs=pl.BlockSpec((ROW_TILE, D_MODEL), row),
        out_shape=jax.ShapeDtypeStruct((n_tok, D_MODEL), F32),
        compiler_params=_params("parallel"),
        name="moe_combine",
    )(yg, yg, route, x1, modt, gn)


def _pack_w_in(w):
    d = w.shape[0]
    qw = N_Q_HEADS * HEAD_DIM
    kw = N_KV_HEADS * HEAD_DIM
    dw = DN_HEADS * DN_D
    o_k = qw
    o_v = o_k + kw
    o_qkv = o_v + kw
    o_z = o_qkv + 3 * dw
    o_ba = o_z + dw
    o_ga = o_ba + 4 * DN_HEADS
    o_gb = o_ga + D_MODEL
    main = jnp.concatenate([w[:, :qw], w[:, o_ga:o_gb], w[:, o_gb:], w[:, o_z:o_ba], w[:, o_qkv:o_z]],
                           axis=1)
    pad = jnp.zeros((d, N_SIDE - 2 * kw - (o_ga - o_ba)), w.dtype)
    side = jnp.concatenate([w[:, o_k:o_v], w[:, o_v:o_qkv], w[:, o_ba:o_ga], pad], axis=1)
    return main.astype(BF16), side.astype(BF16)


def _lane_consts(p):
    out = jnp.zeros((DN_HEADS, 1, LANES), F32)
    out = out.at[:, 0, 2].set(p[0])
    return out.at[:, 0, 3].set(p[1])


def kernel(x_prompt, x_sample, c, cache_k_l0, cache_v_l0, state_delta_l0, cache_k_l1, cache_v_l1,
           state_delta_l1, c_ctx, w_ada, b_ada, norm_g, w_in, attn_sink, conv_w, dn_a_log, dn_dt_bias,
           dn_norm_g, w_pa, w_pb, w_o, w_ff_gate, w_ff_up, w_ff_down, w_router, w_e_gate, w_e_up,
           w_e_down):
    Bc, Tc, _ = x_prompt.shape
    Bl, Tl, _ = x_sample.shape
    n_ctx = Bc * Tc
    n_tok = n_ctx + Bl * Tl
    depth = w_in.shape[0]
    caches = ((cache_k_l0, cache_v_l0, state_delta_l0), (cache_k_l1, cache_v_l1, state_delta_l1))

    x = jnp.concatenate([x_prompt.reshape(n_ctx, D_MODEL), x_sample.reshape(Bl * Tl, D_MODEL)], axis=0)
    cond = jnp.zeros((SUBLANES, D_MODEL), F32).at[:Bl].set(c).at[Bl].set(c_ctx)
    mod = ada_mod(cond, w_ada, b_ada)
    tile_row = np.concatenate([np.full(n_ctx // ROW_TILE, Bl),
                               np.repeat(np.arange(Bl), Tl // ROW_TILE)]).astype(np.int32)
    modt = mod[:, tile_row].reshape(depth, n_tok // ROW_TILE, 6, D_MODEL)
    cos, sin = rope_tables(Tl)

    outs = []
    h = norm_mod(x, norm_g[0, 0:1], modt[0])
    for l in range(depth):
        w_main, w_side = _pack_w_in(w_in[l])
        proj = matmul(h, w_main, BF16)
        side = matmul(h, w_side, F32)
        k_c, v_c, s_c = caches[l]
        oa_c = ctx_attention(proj, side, attn_sink[l], Bc, Tc)
        oa_l = lat_attention(proj, side, attn_sink[l], k_c, v_c, cos, sin, n_ctx, Bl, Tl)
        alog = _lane_consts(dn_a_log[l])
        dtb = _lane_consts(dn_dt_bias[l])
        gdn = dn_norm_g[l].reshape(1, DN_D)
        ob_c, s_fin = deltanet(proj, side, conv_w[l], alog, dtb, gdn, None, 0, Bc, Tc, DN_HPS)
        ob_l = deltanet(proj, side, conv_w[l], alog, dtb, gdn, s_c, n_ctx, Bl, Tl, DN_HPS)
        moe = l % 2 == 1
        i = l // 2
        wr = None
        if moe:
            wr = jnp.zeros((D_MODEL, LANES), F32).at[:, :N_EXPERTS].set(w_router[i])
        res = mix(oa_c, oa_l, ob_c, ob_l, proj, x, modt[l], norm_g[l], w_pa[l].astype(BF16),
                  w_pb[l].astype(BF16), w_o[l].astype(BF16), wr)
        if moe:
            x1, h2, route = res
            pos, n_rows, te, n_used = route_plan(route, MOE_TM)
            xs = dispatch_rows(h2, pos, n_rows)
            ys = moe_grouped(xs, te, n_used, w_e_gate[i], w_e_up[i], w_e_down[i], 512)
            yg = gather_rows(ys, pos)
            x = moe_combine(yg, route, x1, modt[l], norm_g[l])
            h = None if l + 1 == depth else norm_mod(x, norm_g[l + 1, 0:1], modt[l + 1])
        else:
            x1, h2 = res
            nxt = l + 1 < depth
            r = ffn_dense(h2, w_ff_gate[i].astype(BF16), w_ff_up[i].astype(BF16),
                          w_ff_down[i].astype(BF16), x1, modt[l], norm_g[l],
                          modt[l + 1] if nxt else None, norm_g[l + 1] if nxt else None, 1408)
            x, h = (r[0], r[1]) if nxt else (r[0], None)
        k_out = side[:n_ctx, C_K:C_K + N_KV_HEADS * HEAD_DIM].reshape(Bc, Tc, N_KV_HEADS, HEAD_DIM)
        v_out = side[:n_ctx, C_V:C_V + N_KV_HEADS * HEAD_DIM].reshape(Bc, Tc, N_KV_HEADS, HEAD_DIM)
        outs.append((k_out, v_out, s_fin))

    y_prompt = x[:n_ctx].reshape(Bc, Tc, D_MODEL)
    y_sample = x[n_ctx:].reshape(Bl, Tl, D_MODEL)
    return (y_prompt, y_sample, outs[0][0], outs[0][1], outs[0][2], outs[1][0], outs[1][1], outs[1][2])
```

```python
import functools

import numpy as np
import jax
import jax.numpy as jnp
from jax import lax
from jax.experimental import pallas as pl
from jax.experimental.pallas import tpu as pltpu
from jax.experimental.pallas import tpu_sc as plsc

F32 = jnp.float32
BF16 = jnp.bfloat16
HIGHEST = lax.Precision.HIGHEST

D_MODEL = 1024
HEAD_DIM = 128
N_Q_HEADS = 8
N_KV_HEADS = 2
GQA = N_Q_HEADS // N_KV_HEADS
WINDOW = 128
GRID_W = 64
ROPE_BASE = 10000.0
DN_HEADS = 8
DN_D = 128
DN_CONV = 5
DN_CHUNK = 256
DN_LEVELS = (16, 64)
DN_GROUP = 8
DN_CHAINS = 16
N_EXPERTS = 8
TOP_K = 2
EPS = 1e-6
NEG_INF = -1e30

LANES = 128
SUBLANES = 8

ROW_TILE = 256
PROJ_TILE = 1024
FFN_TM = 512
MOE_TM = 1024
MOE_SUB = 512
SC_WINDOW = 32
SC_CORES = 2
SC_SUBCORES = 16

C_Q = 0
C_GA = 1024
C_GB = 2048
C_Z = 3072
C_QB = 4096
C_KB = 5120
C_VB = 6144
N_MAIN = 7168
C_K = 0
C_V = 256
C_BA = 512
N_SIDE = 1024

VMEM_LIMIT = 56 * 1024 * 1024


def _params(*sem):
    return pltpu.CompilerParams(dimension_semantics=sem, vmem_limit_bytes=VMEM_LIMIT)


def _rms(x, g):
    return x * lax.rsqrt(jnp.mean(x * x, axis=-1, keepdims=True) + EPS) * g


def _silu(x):
    return x * jax.nn.sigmoid(x)


def _bdot(a, b):
    return jnp.dot(a.astype(BF16), b.astype(BF16), preferred_element_type=F32)


def _bdot_nt(a, b):
    return lax.dot_general(a.astype(BF16), b.astype(BF16), (((1,), (1,)), ((), ())),
                           preferred_element_type=F32)


def _fdot(a, b):
    return jnp.dot(a, b, precision=HIGHEST, preferred_element_type=F32)


def _pieces(x):
    p1 = x.astype(BF16)
    r1 = x - p1.astype(F32)
    p2 = r1.astype(BF16)
    return p1, p2, (r1 - p2.astype(F32)).astype(BF16)


def _dot_split_rhs(a, b):
    dot = functools.partial(jnp.dot, preferred_element_type=F32)
    b1, b2, b3 = _pieces(b)
    return dot(a, b1) + dot(a, b2) + dot(a, b3)


def _dot_split_both(a, b):
    dot = functools.partial(jnp.dot, preferred_element_type=F32)
    ah = a.astype(BF16)
    al = (a - ah.astype(F32)).astype(BF16)
    bh = b.astype(BF16)
    bl = (b - bh.astype(F32)).astype(BF16)
    return dot(ah, bh) + (dot(ah, bl) + dot(al, bh))


def _dot_split_lhs(a, b):
    dot = functools.partial(jnp.dot, preferred_element_type=F32)
    a1, a2, a3 = _pieces(a)
    return dot(a1, b) + dot(a2, b) + dot(a3, b)


def _ada_kernel(c_ref, w_ref, b_ref, o_ref):
    o_ref[0] = _fdot(_silu(c_ref[...]), w_ref[0]) + b_ref[0]


def ada_mod(cond, w_ada, b_ada):
    L = w_ada.shape[0]
    n = w_ada.shape[2] // D_MODEL
    return pl.pallas_call(
        _ada_kernel,
        grid=(L, n),
        in_specs=[pl.BlockSpec((SUBLANES, D_MODEL), lambda l, j: (0, 0)),
                  pl.BlockSpec((1, D_MODEL, D_MODEL), lambda l, j: (l, 0, j)),
                  pl.BlockSpec((1, 1, D_MODEL), lambda l, j: (l, 0, j))],
        out_specs=pl.BlockSpec((1, SUBLANES, D_MODEL), lambda l, j: (l, 0, j)),
        out_shape=jax.ShapeDtypeStruct((L, SUBLANES, 6 * D_MODEL), F32),
        compiler_params=_params("parallel", "parallel"),
        name="ada_mod",
    )(cond, w_ada, b_ada.reshape(L, 1, 6 * D_MODEL))


def _two_part_specs(n_first, off_first, off_second):
    first = pl.BlockSpec((ROW_TILE, D_MODEL), lambda i: (jnp.minimum(i, n_first - 1) + off_first, 0))
    second = pl.BlockSpec((ROW_TILE, D_MODEL), lambda i: (jnp.maximum(i - n_first, 0) + off_second, 0))
    return first, second


def _norm_mod_kernel(n_first, xa_ref, xb_ref, g_ref, mod_ref, h_ref):
    x = jnp.where(pl.program_id(0) < n_first, xa_ref[...], xb_ref[...])
    h = _rms(x, g_ref[...]) * (1.0 + mod_ref[0, 1:2, :]) + mod_ref[0, 0:1, :]
    h_ref[...] = h.astype(h_ref.dtype)


def norm_mod(xa, xb, g, modt):
    na = xa.shape[0] // ROW_TILE
    n = na + xb.shape[0] // ROW_TILE
    return pl.pallas_call(
        functools.partial(_norm_mod_kernel, na),
        grid=(n,),
        in_specs=[*_two_part_specs(na, 0, 0),
                  pl.BlockSpec((1, D_MODEL), lambda i: (0, 0)),
                  pl.BlockSpec((1, 6, D_MODEL), lambda i: (i, 0, 0))],
        out_specs=pl.BlockSpec((ROW_TILE, D_MODEL), lambda i: (i, 0)),
        out_shape=jax.ShapeDtypeStruct((n * ROW_TILE, D_MODEL), BF16),
        compiler_params=_params("parallel"),
        name="norm_mod",
    )(xa, xb, g, modt)


def _mm_kernel(a_ref, b_ref, o_ref):
    o_ref[...] = jnp.dot(a_ref[...], b_ref[...], preferred_element_type=F32).astype(o_ref.dtype)


def matmul(a, b, out_dtype):
    M, K = a.shape
    N = b.shape[1]
    tm = tn = PROJ_TILE
    return pl.pallas_call(
        _mm_kernel,
        grid=(N // tn, M // tm),
        in_specs=[pl.BlockSpec((tm, K), lambda j, i: (i, 0)),
                  pl.BlockSpec((K, tn), lambda j, i: (0, j))],
        out_specs=pl.BlockSpec((tm, tn), lambda j, i: (i, j)),
        out_shape=jax.ShapeDtypeStruct((M, N), out_dtype),
        compiler_params=_params("parallel", "parallel"),
        name="in_proj",
    )(a, b)


def _ctx_attn_kernel(sink_ref, q_ref, k_ref, v_ref, o_ref):
    hk = pl.program_id(1)
    k = k_ref[...].astype(BF16)
    v = v_ref[...].astype(BF16)
    scale = HEAD_DIM ** -0.5
    for g in range(GQA):
        q = q_ref[:, g * HEAD_DIM:(g + 1) * HEAD_DIM]
        s = _bdot_nt(q, k) * scale
        sink = sink_ref[hk * GQA + g]
        m = jnp.maximum(jnp.max(s, axis=-1, keepdims=True), sink)
        p = jnp.exp(s - m)
        den = jnp.sum(p, axis=-1, keepdims=True) + jnp.exp(sink - m)
        o = _bdot(p, v) / den
        o_ref[:, g * HEAD_DIM:(g + 1) * HEAD_DIM] = o.astype(o_ref.dtype)


def ctx_attention(proj, side, sink, n_seq, T):
    qw = GQA * HEAD_DIM
    return pl.pallas_call(
        _ctx_attn_kernel,
        grid_spec=pltpu.PrefetchScalarGridSpec(
            num_scalar_prefetch=1,
            grid=(n_seq, N_KV_HEADS),
            in_specs=[pl.BlockSpec((T, qw), lambda b, h, s: (b, C_Q // qw + h)),
                      pl.BlockSpec((T, HEAD_DIM), lambda b, h, s: (b, C_K // HEAD_DIM + h)),
                      pl.BlockSpec((T, HEAD_DIM), lambda b, h, s: (b, C_V // HEAD_DIM + h))],
            out_specs=pl.BlockSpec((T, qw), lambda b, h, s: (b, h))),
        out_shape=jax.ShapeDtypeStruct((n_seq * T, N_Q_HEADS * HEAD_DIM), BF16),
        compiler_params=_params("parallel", "parallel"),
        name="ctx_attn",
    )(sink, proj, side, side)


def _rope(x, cos, sin):
    lane = lax.broadcasted_iota(jnp.int32, x.shape, 1)
    q = HEAD_DIM // 4
    swapped = jnp.where((lane % (2 * q)) < q,
                        pltpu.roll(x, HEAD_DIM - q, 1), pltpu.roll(x, q, 1))
    return x * cos + swapped * sin


def _lat_attn_kernel(T, sink_ref, q_ref, k_ref, v_ref, kc_ref, vc_ref, cosq_ref, sinq_ref,
                     cos_ref, sin_ref, o_ref, kpad, vpad):
    hk = pl.program_id(1)
    i = pl.program_id(2)
    B = WINDOW

    @pl.when(i == 0)
    def _():
        zeros = jnp.zeros((B, HEAD_DIM), BF16)
        kpad[0:B, :] = zeros
        kpad[T + B:T + 2 * B, :] = zeros
        vpad[0:B, :] = zeros
        vpad[T + B:T + 2 * B, :] = zeros
        kpad[B:T + B, :] = _rope(k_ref[...], cos_ref[...], sin_ref[...]).astype(BF16)
        vpad[B:T + B, :] = v_ref[...].astype(BF16)

    start = pl.multiple_of(i * B, B)
    kw = kpad[pl.ds(start, 3 * B), :]
    vw = vpad[pl.ds(start, 3 * B), :]
    kc = kc_ref[0].astype(BF16)
    vc = vc_ref[0].astype(BF16)

    rows = GQA * B
    qi = lax.broadcasted_iota(jnp.int32, (rows, B), 0) % B
    kj = lax.broadcasted_iota(jnp.int32, (rows, B), 1)
    mask_prev = (kj >= qi) & (i > 0)
    mask_next = (kj <= qi) & (i < pl.num_programs(2) - 1)

    cq = cosq_ref[...]
    sq = sinq_ref[...]
    scale = HEAD_DIM ** -0.5
    q = jnp.concatenate(
        [_rope(q_ref[:, g * HEAD_DIM:(g + 1) * HEAD_DIM].astype(F32), cq, sq) for g in range(GQA)],
        axis=0) * scale
    s_c = _bdot_nt(q, kc)
    s_w = _bdot_nt(q, kw)
    s_w = jnp.concatenate([jnp.where(mask_prev, s_w[:, 0:B], NEG_INF), s_w[:, B:2 * B],
                           jnp.where(mask_next, s_w[:, 2 * B:3 * B], NEG_INF)], axis=1)
    gi = lax.broadcasted_iota(jnp.int32, (rows, 1), 0) // B
    sink = jnp.zeros((rows, 1), F32)
    for g in range(GQA):
        sink = jnp.where(gi == g, sink_ref[hk * GQA + g], sink)
    m = jnp.maximum(jnp.maximum(jnp.max(s_c, axis=-1, keepdims=True),
                                jnp.max(s_w, axis=-1, keepdims=True)), sink)
    p_c = jnp.exp(s_c - m)
    p_w = jnp.exp(s_w - m)
    den = (jnp.sum(p_c, axis=-1, keepdims=True) + jnp.sum(p_w, axis=-1, keepdims=True)
           + jnp.exp(sink - m))
    o = (_bdot(p_c, vc) + _bdot(p_w, vw)) / den
    for g in range(GQA):
        o_ref[:, g * HEAD_DIM:(g + 1) * HEAD_DIM] = o[g * B:(g + 1) * B, :].astype(o_ref.dtype)


def lat_attention(proj, side, sink, cache_k, cache_v, cos, sin, row0, n_seq, T):
    qw = GQA * HEAD_DIM
    B = WINDOW
    nb = T // B
    P = cache_k.shape[1]
    kc = cache_k.reshape(n_seq, P, N_KV_HEADS * HEAD_DIM)
    vc = cache_v.reshape(n_seq, P, N_KV_HEADS * HEAD_DIM)
    rb = row0 // B
    rt = row0 // T
    return pl.pallas_call(
        functools.partial(_lat_attn_kernel, T),
        grid_spec=pltpu.PrefetchScalarGridSpec(
            num_scalar_prefetch=1,
            grid=(n_seq, N_KV_HEADS, nb),
            in_specs=[pl.BlockSpec((B, qw), lambda b, h, i, s: (rb + b * nb + i, C_Q // qw + h)),
                      pl.BlockSpec((T, HEAD_DIM), lambda b, h, i, s: (rt + b, C_K // HEAD_DIM + h)),
                      pl.BlockSpec((T, HEAD_DIM), lambda b, h, i, s: (rt + b, C_V // HEAD_DIM + h)),
                      pl.BlockSpec((1, P, HEAD_DIM), lambda b, h, i, s: (b, 0, h)),
                      pl.BlockSpec((1, P, HEAD_DIM), lambda b, h, i, s: (b, 0, h)),
                      pl.BlockSpec((B, HEAD_DIM), lambda b, h, i, s: (i, 0)),
                      pl.BlockSpec((B, HEAD_DIM), lambda b, h, i, s: (i, 0)),
                      pl.BlockSpec((T, HEAD_DIM), lambda b, h, i, s: (0, 0)),
                      pl.BlockSpec((T, HEAD_DIM), lambda b, h, i, s: (0, 0))],
            out_specs=pl.BlockSpec((B, qw), lambda b, h, i, s: (b * nb + i, h)),
            scratch_shapes=[pltpu.VMEM((T + 2 * B, HEAD_DIM), BF16),
                            pltpu.VMEM((T + 2 * B, HEAD_DIM), BF16)]),
        out_shape=jax.ShapeDtypeStruct((n_seq * T, N_Q_HEADS * HEAD_DIM), BF16),
        compiler_params=_params("parallel", "parallel", "arbitrary"),
        name="lat_attn",
    )(sink, proj, side, side, kc, vc, cos, sin, cos, sin)


def rope_tables(T):
    rows = (np.arange(T) // GRID_W).astype(np.float32)
    cols = (np.arange(T) % GRID_W).astype(np.float32)
    n_freq = HEAD_DIM // 4
    inv = jnp.asarray(ROPE_BASE, F32) ** (-jnp.arange(n_freq, dtype=F32) / n_freq)
    ar = jnp.asarray(rows)[:, None] * inv
    ac = jnp.asarray(cols)[:, None] * inv
    cos = jnp.concatenate([jnp.cos(ar), jnp.cos(ar), jnp.cos(ac), jnp.cos(ac)], axis=-1)
    sin = jnp.concatenate([-jnp.sin(ar), jnp.sin(ar), -jnp.sin(ac), jnp.sin(ac)], axis=-1)
    return cos, sin


def _dn_kernel(T, hps, has_s0, *refs):
    n_in = 12 if has_s0 else 11
    (q_ref, k_ref, v_ref, z_ref, ba_ref, wq_ref, wk_ref, wv_ref, alog_ref, dt_ref, gn_ref) = refs[:11]
    s0_ref = refs[11] if has_s0 else None
    if has_s0:
        o_ref = refs[n_in]
        sfin_ref = None
        scratch = refs[n_in + 1:]
    else:
        o_ref, sfin_ref = refs[n_in], refs[n_in + 1]
        scratch = refs[n_in + 2:]
    xpad, qs, ks, vs, cols, gcs, gcT, kTs, U, WQ, QKM, KDT, EGL, OST = scratch

    hg = pl.program_id(1)
    C = DN_CHUNK
    D = DN_D
    n = T // C
    half = DN_CONV // 2
    PAD = SUBLANES

    zpad = jnp.zeros((PAD, D), F32)
    xpad[0:PAD, :] = zpad
    xpad[T + PAD:T + 2 * PAD, :] = zpad

    def conv_silu(x_ref, w_ref, hh):
        xpad[PAD:T + PAD, :] = x_ref[:, hh * D:(hh + 1) * D].astype(F32)
        w = w_ref[:, hh * D:(hh + 1) * D]
        acc = xpad[pl.ds(PAD - half, T), :] * w[0:1, :]
        for j in range(1, DN_CONV):
            acc = acc + xpad[pl.ds(PAD - half + j, T), :] * w[j:j + 1, :]
        return _silu(acc)

    def l2n(x):
        return x * lax.rsqrt(jnp.sum(x * x, axis=-1, keepdims=True) + EPS)

    ri = lax.broadcasted_iota(jnp.int32, (C, C), 0)
    ci = lax.broadcasted_iota(jnp.int32, (C, C), 1)
    eye = jnp.where(ri == ci, 1.0, 0.0).astype(F32)
    same_blk = {s: (ri // s) == (ci // s) for s in DN_LEVELS}
    lower =jnp.where(ri >= ci, 1.0, 0.0).astype(BF16)
    li = lax.broadcasted_iota(jnp.int32, (LANES, LANES), 0)
    lj = lax.broadcasted_iota(jnp.int32, (LANES, LANES), 1)
    lane_t = lax.broadcasted_iota(jnp.int32, (T, LANES), 1)
    lane_c = lax.broadcasted_iota(jnp.int32, (C, LANES), 1)

    def prologue(hh):
        def conv_q():
            qs[hh] = l2n(conv_silu(q_ref, wq_ref, hh)) * (D ** -0.5)

        def conv_k():
            ks[hh] = l2n(conv_silu(k_ref, wk_ref, hh))

        def conv_v():
            vs[hh] = conv_silu(v_ref, wv_ref, hh)

        def gates():
            head = hg * hps + hh
            sel = jnp.where((lj < 4) & (li == lj * DN_HEADS + head), 1.0, 0.0).astype(BF16)
            raw = _dot_split_lhs(ba_ref[...], sel)
            gate = -jnp.exp(alog_ref[hh]) * jax.nn.softplus(raw + dt_ref[hh])
            cols[hh] = jnp.where(lane_t < 2, jax.nn.sigmoid(raw), gate)

        def cumulative(c):
            rows = slice(c * C, (c + 1) * C)
            g = cols[hh, rows, :]
            pre = _dot_split_rhs(lower, g)
            gc = jnp.where(lane_c == 3, pre[C - 1:C, :] - pre + g, pre)
            gcs[hh, rows, :] = gc
            gcT[hh * n + c] = gc.T[0:SUBLANES, :]
            kTs[hh * n + c] = ks[hh, rows, :].T

        return [conv_q, conv_k, conv_v, gates] + [functools.partial(cumulative, c) for c in range(n)]

    chains = [(hh, c, d) for hh in range(hps) for c in range(n) for d in (0, 1)]
    chain_id = {ch: j for j, ch in enumerate(chains)}

    def nilpotent_inverse(ms, squarings):
        ts = [eye + m for m in ms]
        pws = ms
        for _ in range(squarings):
            pws = [_bdot(pw, pw) for pw in pws]
            ts = [t + _bdot(t, pw) for t, pw in zip(ts, pws)]
            yield
        return ts

    def phase_a(j0, grp):
        G = range(len(grp))
        rows = [slice(c * C, (c + 1) * C) for (_, c, _) in grp]
        incl = [(ri >= ci) if d == 0 else (ri <= ci) for (_, _, d) in grp]
        strict = [(ri > ci) if d == 0 else (ri < ci) for (_, _, d) in grp]
        q = [qs[grp[i][0], rows[i], :] for i in G]
        k = [ks[grp[i][0], rows[i], :] for i in G]
        v = [vs[grp[i][0], rows[i], :] for i in G]
        gcc = [gcs[hh, rows[i], 2 + d:3 + d] for i, (hh, _, d) in enumerate(grp)]
        bcol = [cols[hh, rows[i], d:d + 1] for i, (hh, _, d) in enumerate(grp)]
        kT = [kTs[hh * n + c] for (hh, c, _) in grp]
        gcr = [gcT[hh * n + c, 2 + d:3 + d, :] for (hh, c, d) in grp]
        g_last = [gcr[i][:, C - 1:C] if grp[i][2] == 0 else gcr[i][:, 0:1] for i in G]
        decay = [jnp.where(incl[i], jnp.exp(jnp.where(incl[i], gcc[i] - gcr[i], 0.0)), 0.0) for i in G]
        qkk_of = {}
        for i in G:
            if grp[i][:2] not in qkk_of:
                qkk_of[grp[i][:2]] = _bdot(jnp.concatenate([q[i], k[i]], axis=0), kT[i])
        qkk = [qkk_of[grp[i][:2]] for i in G]
        a = [jnp.where(strict[i], qkk[i][C:2 * C, :] * bcol[i] * decay[i], 0.0) for i in G]
        yield
        sizes = DN_LEVELS + (C,)
        t_inv = yield from nilpotent_inverse([jnp.where(same_blk[sizes[0]], -a[i], 0.0) for i in G],
                                             sizes[0].bit_length() - 2)
        for prev, size in zip(sizes[:-1], sizes[1:]):
            off = [jnp.where(same_blk[prev], 0.0, -a[i]) for i in G]
            if size < C:
                off = [jnp.where(same_blk[size], off[i], 0.0) for i in G]
            e_neg = [_bdot(t_inv[i], off[i]) for i in G]
            yield
            f = yield from nilpotent_inverse(e_neg, (size // prev).bit_length() - 2)
            t_inv = [_bdot(f[i], t_inv[i]) for i in G]
            yield
        egc = [jnp.exp(gcc[i]) for i in G]
        uw = [_bdot(t_inv[i], jnp.concatenate([v[i] * bcol[i], k[i] * (bcol[i] * egc[i])], axis=1))
              for i in G]
        r0, r1 = j0, j0 + len(grp)
        U[r0 * C:r1 * C, :] = jnp.concatenate([uw[i][:, 0:D] for i in G], axis=0)
        WQ[r0 * 2 * C:r1 * 2 * C, :] = jnp.concatenate(
            [jnp.concatenate([uw[i][:, D:2 * D], q[i] * egc[i]], axis=0) for i in G], axis=0).astype(BF16)
        QKM[r0 * C:r1 * C, :] = jnp.concatenate(
            [qkk[i][0:C, :] * decay[i] for i in G], axis=0).astype(BF16)
        KDT[r0 * D:r1 * D, :] = jnp.concatenate(
            [kT[i] * jnp.exp(g_last[i] - gcr[i]) for i in G], axis=0).astype(BF16)
        EGL[r0 * SUBLANES:r1 * SUBLANES, :] = jnp.concatenate(
            [jnp.broadcast_to(jnp.exp(g_last[i]), (SUBLANES, LANES)) for i in G], axis=0)

    for hh in range(hps):
        for step in prologue(hh):
            step()
    for j0 in range(0, len(chains), DN_GROUP):
        for _ in phase_a(j0, chains[j0:j0 + DN_GROUP]):
            pass

    if has_s0:
        state = [s0_ref[0, d, hh] for hh in range(hps) for d in (0, 1)]
    else:
        state = [jnp.zeros((D, D), F32) for _ in range(2 * hps)]
    for i in range(n):
        outs = []
        new_state = []
        for hh in range(hps):
            for d in (0, 1):
                j = chain_id[(hh, i if d == 0 else n - 1 - i, d)]
                s_old = state[hh * 2 + d]
                xs = jnp.dot(WQ[j * 2 * C:(j + 1) * 2 * C, :], s_old.astype(BF16),
                             preferred_element_type=F32)
                v_new = (U[j * C:(j + 1) * C, :] - xs[0:C, :]).astype(BF16)
                outs.append(xs[C:2 * C, :] + jnp.dot(QKM[j * C:(j + 1) * C, :], v_new,
                                                     preferred_element_type=F32))
                new_state.append(s_old * EGL[j * SUBLANES:j * SUBLANES + 1, 0:1]
                                 + jnp.dot(KDT[j * D:(j + 1) * D, :], v_new, preferred_element_type=F32))
        OST[i] = jnp.concatenate(outs, axis=0)
        state = new_state

    if not has_s0:
        for hh in range(hps):
            for d in (0, 1):
                sfin_ref[0, d, hh] = state[hh * 2 + d]

    for hh in range(hps):
        for c in range(n):
            rows = slice(c * C, (c + 1) * C)
            o = (OST[c, (2 * hh) * C:(2 * hh + 1) * C, :]
                 + OST[n - 1 - c, (2 * hh + 1) * C:(2 * hh + 2) * C, :])
            y = _rms(o, gn_ref[...]) * _silu(z_ref[rows, hh * D:(hh + 1) * D].astype(F32))
            o_ref[rows, hh * D:(hh + 1) * D] = y.astype(o_ref.dtype)


def deltanet(proj, side, conv_w, alog_l, dt_l, gn, s0, row0, n_seq, T, hps):
    rt = row0 // T
    has_s0 = s0 is not None
    W = hps * DN_D
    n = T // DN_CHUNK
    n_chain = hps * n * 2

    def col(c0):
        return pl.BlockSpec((T, W), lambda b, h: (rt + b, c0 // W + h))

    def wcol(c0):
        return pl.BlockSpec((DN_CONV, W), lambda b, h: (0, c0 // W + h))

    in_specs = [col(C_QB), col(C_KB), col(C_VB), col(C_Z),
                pl.BlockSpec((T, LANES), lambda b, h: (rt + b, C_BA // LANES)),
                wcol(0), wcol(DN_HEADS * DN_D), wcol(2 * DN_HEADS * DN_D),
                pl.BlockSpec((hps, 1, LANES), lambda b, h: (h, 0, 0)),
                pl.BlockSpec((hps, 1, LANES), lambda b, h: (h, 0, 0)),
                pl.BlockSpec((1, DN_D), lambda b, h: (0, 0))]
    args = [proj, proj, proj, proj, side, conv_w, conv_w, conv_w, alog_l, dt_l, gn]
    o_spec = pl.BlockSpec((T, W), lambda b, h: (b, h))
    o_shape = jax.ShapeDtypeStruct((n_seq * T, DN_HEADS * DN_D), BF16)
    s_spec = pl.BlockSpec((1, 2, hps, DN_D, DN_D), lambda b, h: (b, 0, h, 0, 0))
    if has_s0:
        in_specs.append(s_spec)
        args.append(s0)
        out_specs, out_shape = o_spec, o_shape
    else:
        out_specs = (o_spec, s_spec)
        out_shape = (o_shape, jax.ShapeDtypeStruct((n_seq, 2, DN_HEADS, DN_D, DN_D), F32))
    C = DN_CHUNK
    scratch = [pltpu.VMEM((T + 2 * SUBLANES, DN_D), F32),
               pltpu.VMEM((hps, T, DN_D), F32),
               pltpu.VMEM((hps, T, DN_D), F32),
               pltpu.VMEM((hps, T, DN_D), F32),
               pltpu.VMEM((hps, T, LANES), F32),
               pltpu.VMEM((hps, T, LANES), F32),
               pltpu.VMEM((hps * n, SUBLANES, C), F32),
               pltpu.VMEM((hps * n, DN_D, C), F32),
               pltpu.VMEM((n_chain * C, DN_D), F32),
               pltpu.VMEM((n_chain * 2 * C, DN_D), BF16),
               pltpu.VMEM((n_chain * C, C), BF16),
               pltpu.VMEM((n_chain * DN_D, C), BF16),
               pltpu.VMEM((n_chain * SUBLANES, LANES), F32),
               pltpu.VMEM((n, 2 * hps * C, DN_D), F32)]
    return pl.pallas_call(
        functools.partial(_dn_kernel, T, hps, has_s0),
        grid=(n_seq, DN_HEADS // hps),
        in_specs=in_specs,
        out_specs=out_specs,
        out_shape=out_shape,
        scratch_shapes=scratch,
        compiler_params=_params("parallel", "parallel"),
        name="deltanet_lat" if has_s0 else "deltanet_ctx",
    )(*args)


def _route_top2(logits):
    lane = lax.broadcasted_iota(jnp.int32, logits.shape, 1).astype(F32)
    lg = jnp.where(lane < N_EXPERTS, logits, -jnp.inf)
    m1 = jnp.max(lg, axis=-1, keepdims=True)
    i1 = jnp.min(jnp.where(lg == m1, lane, float(LANES)), axis=-1, keepdims=True)
    lg2 = jnp.where(lane == i1, -jnp.inf, lg)
    m2 = jnp.max(lg2, axis=-1, keepdims=True)
    i2 = jnp.min(jnp.where(lg2 == m2, lane, float(LANES)), axis=-1, keepdims=True)
    e2 = jnp.exp(m2 - m1)
    den = 1.0 + e2
    return jnp.where(lane == 0.0, i1, jnp.where(lane == 1.0, i2,
                     jnp.where(lane == 2.0, 1.0 / den, jnp.where(lane == 3.0, e2 / den, 0.0))))


def _mix_kernel(route, n_ctx_tiles, oac_ref, oal_ref, obc_ref, obl_ref, ga_ref, gb_ref, xc_ref,
                xl_ref, mod_ref, gn_ref, wpa_ref, wpb_ref, wo_ref, *rest):
    if route:
        wr_ref, x1_ref, h2_ref, rt_ref = rest
    else:
        x1_ref, h2_ref = rest
    is_ctx = pl.program_id(0) < n_ctx_tiles
    o_a = jnp.where(is_ctx, oac_ref[...], oal_ref[...])
    o_b = jnp.where(is_ctx, obc_ref[...], obl_ref[...])
    x = jnp.where(is_ctx, xc_ref[...], xl_ref[...])
    br_a = jnp.dot(o_a, wpa_ref[...], preferred_element_type=F32)
    br_b = jnp.dot(o_b, wpb_ref[...], preferred_element_type=F32)
    merged = (jax.nn.sigmoid(ga_ref[...].astype(F32)) * br_a
              + jax.nn.sigmoid(gb_ref[...].astype(F32)) * br_b)
    y = jnp.dot(merged.astype(BF16), wo_ref[...], preferred_element_type=F32)
    x1 = x + mod_ref[0, 2:3, :] * _rms(y, gn_ref[1:2, :])
    x1_ref[...] = x1
    h2 = _rms(x1, gn_ref[2:3, :]) * (1.0 + mod_ref[0, 4:5, :]) + mod_ref[0, 3:4, :]
    h2_ref[...] = h2.astype(h2_ref.dtype)
    if route:
        rt_ref[...] = _route_top2(_dot_split_both(h2, wr_ref[...]))


def mix(oa_c, oa_l, ob_c, ob_l, proj, x_c, x_l, x_l_tile0, modt, gn, wpa, wpb, wo, w_router):
    nc = oa_c.shape[0] // ROW_TILE
    nl = oa_l.shape[0] // ROW_TILE
    n = nc + nl
    n_tok = n * ROW_TILE
    route = w_router is not None
    row = lambda i: (i, 0)
    full = lambda i: (0, 0)
    crow, lrow = _two_part_specs(nc, 0, 0)
    in_specs = [crow, lrow, crow, lrow,
                pl.BlockSpec((ROW_TILE, D_MODEL), lambda i: (i, C_GA // D_MODEL)),
                pl.BlockSpec((ROW_TILE, D_MODEL), lambda i: (i, C_GB // D_MODEL)),
                *_two_part_specs(nc, 0, x_l_tile0),
                pl.BlockSpec((1, 6, D_MODEL), lambda i: (i, 0, 0)),
                pl.BlockSpec((4, D_MODEL), full),
                pl.BlockSpec((D_MODEL, D_MODEL), full),
                pl.BlockSpec((D_MODEL, D_MODEL), full),
                pl.BlockSpec((D_MODEL, D_MODEL), full)]
    args = [oa_c, oa_l, ob_c, ob_l, proj, proj, x_c, x_l, modt, gn, wpa, wpb, wo]
    out_specs = [pl.BlockSpec((ROW_TILE, D_MODEL), row), pl.BlockSpec((ROW_TILE, D_MODEL), row)]
    out_shape = [jax.ShapeDtypeStruct((n_tok, D_MODEL), F32),
                 jax.ShapeDtypeStruct((n_tok, D_MODEL), F32 if route else BF16)]
    if route:
        in_specs.append(pl.BlockSpec((D_MODEL, LANES), full))
        args.append(w_router)
        out_specs.append(pl.BlockSpec((ROW_TILE, LANES), row))
        out_shape.append(jax.ShapeDtypeStruct((n_tok, LANES), F32))
    return pl.pallas_call(
        functools.partial(_mix_kernel, route, nc),
        grid=(n,),
        in_specs=in_specs,
        out_specs=tuple(out_specs),
        out_shape=tuple(out_shape),
        compiler_params=_params("parallel"),
        name="mix_route" if route else "mix",
    )(*args)


def _ffn_kernel(next_norm, h_ref, wg_ref, wu_ref, wd_ref, x_ref, mod_ref, gn_ref, *rest):
    if next_norm:
        modn_ref, gnn_ref, x2_ref, hn_ref, acc = rest
    else:
        x2_ref, acc = rest
    f = pl.program_id(1)
    h = h_ref[...]
    gate = jnp.dot(h, wg_ref[...], preferred_element_type=F32)
    up = jnp.dot(h, wu_ref[...], preferred_element_type=F32)
    part = jnp.dot((_silu(gate) * up).astype(BF16), wd_ref[...], preferred_element_type=F32)
    acc[...] = jnp.where(f == 0, 0.0, acc[...]) + part

    @pl.when(f == pl.num_programs(1) - 1)
    def _():
        for r in range(FFN_TM // ROW_TILE):
            rs = slice(r * ROW_TILE, (r + 1) * ROW_TILE)
            x2 = x_ref[rs, :] + mod_ref[r, 5:6, :] * _rms(acc[rs, :], gn_ref[3:4, :])
            x2_ref[rs, :] = x2
            if next_norm:
                hn = _rms(x2, gnn_ref[0:1, :]) * (1.0 + modn_ref[r, 1:2, :]) + modn_ref[r, 0:1, :]
                hn_ref[rs, :] = hn.astype(hn_ref.dtype)


def ffn_dense(h2, wg, wu, wd, x1, modt, gn, modt_next, gn_next, tf):
    n_tok = x1.shape[0]
    F = wg.shape[1]
    nr = FFN_TM // ROW_TILE
    next_norm = modt_next is not None
    row = lambda i, f: (i, 0)
    in_specs = [pl.BlockSpec((FFN_TM, D_MODEL), row),
                pl.BlockSpec((D_MODEL, tf), lambda i, f: (0, f)),
                pl.BlockSpec((D_MODEL, tf), lambda i, f: (0, f)),
                pl.BlockSpec((tf, D_MODEL), lambda i, f: (f, 0)),
                pl.BlockSpec((FFN_TM, D_MODEL), row),
                pl.BlockSpec((nr, 6, D_MODEL), lambda i, f: (i, 0, 0)),
                pl.BlockSpec((4, D_MODEL), lambda i, f: (0, 0))]
    args = [h2, wg, wu, wd, x1, modt, gn]
    out_specs = [pl.BlockSpec((FFN_TM, D_MODEL), row)]
    out_shape = [jax.ShapeDtypeStruct((n_tok, D_MODEL), F32)]
    if next_norm:
        in_specs += [pl.BlockSpec((nr, 6, D_MODEL), lambda i, f: (i, 0, 0)),
                     pl.BlockSpec((4, D_MODEL), lambda i, f: (0, 0))]
        args += [modt_next, gn_next]
        out_specs.append(pl.BlockSpec((FFN_TM, D_MODEL), row))
        out_shape.append(jax.ShapeDtypeStruct((n_tok, D_MODEL), BF16))
    return pl.pallas_call(
        functools.partial(_ffn_kernel, next_norm),
        grid=(n_tok // FFN_TM, F // tf),
        in_specs=in_specs,
        out_specs=tuple(out_specs),
        out_shape=tuple(out_shape),
        scratch_shapes=[pltpu.VMEM((FFN_TM, D_MODEL), F32)],
        compiler_params=_params("parallel", "arbitrary"),
        name="ffn_dense",
    )(*args)


def gather_rows(x, idx):
    n = idx.shape[0]
    w = x.shape[1]
    per_worker = n // (SC_CORES * SC_SUBCORES)
    steps = per_worker // SC_WINDOW
    assert steps * SC_WINDOW * SC_CORES * SC_SUBCORES == n
    mesh = plsc.VectorSubcoreMesh(core_axis_name="core", subcore_axis_name="subcore")

    @functools.partial(pl.kernel, out_type=jax.ShapeDtypeStruct((n, w), x.dtype), mesh=mesh,
                       scratch_types=[pltpu.VMEM((SC_WINDOW,), jnp.int32),
                                      pltpu.VMEM((SC_WINDOW, w), x.dtype),
                                      pltpu.SemaphoreType.DMA])
    def gather_kernel(x_hbm, i_hbm, o_hbm, idx_v, rows_v, sem):
        worker = lax.axis_index("subcore") * SC_CORES + lax.axis_index("core")
        base = worker * per_worker

        @pl.loop(0, steps)
        def _(s):
            off = base + s * SC_WINDOW
            pltpu.sync_copy(i_hbm.at[pl.ds(off, SC_WINDOW)], idx_v)
            pltpu.async_copy(x_hbm.at[idx_v], rows_v, sem).wait()
            pltpu.sync_copy(rows_v, o_hbm.at[pl.ds(off, SC_WINDOW)])

    return gather_kernel(x, idx)


def dispatch_rows(x, pos, n_rows):
    n_tok, w = x.shape
    per_worker = n_tok // (SC_CORES * SC_SUBCORES)
    steps = per_worker // SC_WINDOW
    assert steps * SC_WINDOW * SC_CORES * SC_SUBCORES == n_tok
    mesh = plsc.VectorSubcoreMesh(core_axis_name="core", subcore_axis_name="subcore")

    @functools.partial(pl.kernel, out_type=jax.ShapeDtypeStruct((n_rows, w), x.dtype), mesh=mesh,
                       scratch_types=[pltpu.VMEM((SC_WINDOW,), jnp.int32),
                                      pltpu.VMEM((SC_WINDOW, w), x.dtype)])
    def dispatch_kernel(x_hbm, p_hbm, o_hbm, idx_v, rows_v):
        worker = lax.axis_index("subcore") * SC_CORES + lax.axis_index("core")
        base = worker * per_worker

        @pl.loop(0, steps)
        def _(s):
            off = base + s * SC_WINDOW
            pltpu.sync_copy(x_hbm.at[pl.ds(off, SC_WINDOW)], rows_v)
            for k in range(TOP_K):
                pltpu.sync_copy(p_hbm.at[pl.ds(k * n_tok + off, SC_WINDOW)], idx_v)
                pltpu.sync_copy(rows_v, o_hbm.at[idx_v])

    return dispatch_kernel(x, pos)


def route_plan(route, tm):
    n_tok = route.shape[0]
    ef = route[:, 0:TOP_K].astype(jnp.int32).T.reshape(-1)
    onehot = (ef[:, None] == jnp.arange(N_EXPERTS, dtype=jnp.int32)[None, :]).astype(jnp.int32)
    csum = jnp.cumsum(onehot, axis=0)
    rank = jnp.sum(csum * onehot, axis=1) - 1
    counts = csum[-1]
    padded = ((counts + tm - 1) // tm) * tm
    ends = jnp.cumsum(padded)
    pos = (ends - padded)[ef] + rank
    n_rows = TOP_K * n_tok + N_EXPERTS * tm
    n_tiles = n_rows // tm
    n_used = (ends[-1] // tm).astype(jnp.int32)
    tile_start = jnp.arange(n_tiles, dtype=jnp.int32) * tm
    te = jnp.sum((tile_start[:, None] >= ends[None, :]).astype(jnp.int32), axis=1)
    te = jnp.where(jnp.arange(n_tiles) < n_used, te, te[n_used - 1])
    real_end = ends - padded + counts
    tv = jnp.clip(real_end[te] - tile_start, 0, tm)
    tv = jnp.where(jnp.arange(n_tiles) < n_used, tv, 0)
    return pos.astype(jnp.int32), n_rows, te.astype(jnp.int32), tv.astype(jnp.int32), n_used.reshape(1)


def _moe_kernel(te_ref, tv_ref, nu_ref, x_ref, wg_ref, wu_ref, wd_ref, y_ref, acc):
    i = pl.program_id(0)
    f = pl.program_id(1)
    last = f == pl.num_programs(1) - 1
    used = i < nu_ref[0]
    valid = tv_ref[i]
    n_sub = MOE_TM // MOE_SUB

    @pl.when(used)
    def _():
        wg = wg_ref[0].astype(BF16)
        wu = wu_ref[0].astype(BF16)
        wd = wd_ref[0].astype(BF16)

        def sub_tile(r):
            rows = slice(r * MOE_SUB, (r + 1) * MOE_SUB)
            x = x_ref[rows, :].astype(BF16)
            gate = jnp.dot(x, wg, preferred_element_type=F32)
            up = jnp.dot(x, wu, preferred_element_type=F32)
            part = jnp.dot((_silu(gate) * up).astype(BF16), wd, preferred_element_type=F32)
            total = jnp.where(f == 0, 0.0, acc[rows, :]) + part
            acc[rows, :] = total
            y_ref[rows, :] = total

        sub_tile(0)
        for r in range(1, n_sub):
            pl.when(valid > r * MOE_SUB)(functools.partial(sub_tile, r))

    def zero_rows(r):
        y_ref[r * MOE_SUB:(r + 1) * MOE_SUB, :] = jnp.zeros((MOE_SUB, D_MODEL), y_ref.dtype)

    for r in range(n_sub):
        pl.when(last & (valid <= r * MOE_SUB))(functools.partial(zero_rows, r))


def moe_grouped(xs, te, tv, n_used, wg, wu, wd, tf):
    n_rows = xs.shape[0]
    E, _, F = wg.shape
    nf = F // tf

    def xrow(i, f, te, tv, nu):
        return (jnp.minimum(i, nu[0] - 1), 0)

    def fcol(i, f, nu):
        return jnp.where(i < nu[0], f, nf - 1)

    return pl.pallas_call(
        _moe_kernel,
        grid_spec=pltpu.PrefetchScalarGridSpec(
            num_scalar_prefetch=3,
            grid=(n_rows // MOE_TM, nf),
            in_specs=[pl.BlockSpec((MOE_TM, D_MODEL), xrow),
                      pl.BlockSpec((1, D_MODEL, tf),
                                   lambda i, f, te, tv, nu: (te[i], 0, fcol(i, f, nu))),
                      pl.BlockSpec((1, D_MODEL, tf),
                                   lambda i, f, te, tv, nu: (te[i], 0, fcol(i, f, nu))),
                      pl.BlockSpec((1, tf, D_MODEL),
                                   lambda i, f, te, tv, nu: (te[i], fcol(i, f, nu), 0))],
            out_specs=pl.BlockSpec((MOE_TM, D_MODEL), lambda i, f, te, tv, nu: (i, 0)),
            scratch_shapes=[pltpu.VMEM((MOE_TM, D_MODEL), F32)]),
        out_shape=jax.ShapeDtypeStruct((n_rows, D_MODEL), F32),
        compiler_params=_params("arbitrary", "arbitrary"),
        name="moe_grouped",
    )(te, tv, n_used, xs, wg, wu, wd)


def _combine_kernel(y0_ref, y1_ref, rt_ref, x_ref, mod_ref, gn_ref, x2_ref):
    f = rt_ref[:, 2:3] * y0_ref[...] + rt_ref[:, 3:4] * y1_ref[...]
    x2_ref[...] = x_ref[...] + mod_ref[0, 5:6, :] * _rms(f, gn_ref[3:4, :])


def moe_combine(yg, route, x1, modt, gn):
    n_tok = x1.shape[0]
    n = n_tok // ROW_TILE
    row = lambda i: (i, 0)
    return pl.pallas_call(
        _combine_kernel,
        grid=(n,),
        in_specs=[pl.BlockSpec((ROW_TILE, D_MODEL), row),
                  pl.BlockSpec((ROW_TILE, D_MODEL), lambda i: (n + i, 0)),
                  pl.BlockSpec((ROW_TILE, LANES), row),
                  pl.BlockSpec((ROW_TILE, D_MODEL), row),
                  pl.BlockSpec((1, 6, D_MODEL), lambda i: (i, 0, 0)),
                  pl.BlockSpec((4, D_MODEL), lambda i: (0, 0))],
        out_specs=pl.BlockSpec((ROW_TILE, D_MODEL), row),
        out_shape=jax.ShapeDtypeStruct((n_tok, D_MODEL), F32),
        compiler_params=_params("parallel"),
        name="moe_combine",
    )(yg, yg, route, x1, modt, gn)


def _pack_w_in(w):
    d = w.shape[0]
    qw = N_Q_HEADS * HEAD_DIM
    kw = N_KV_HEADS * HEAD_DIM
    dw = DN_HEADS * DN_D
    o_k = qw
    o_v = o_k + kw
    o_qkv = o_v + kw
    o_z = o_qkv + 3 * dw
    o_ba = o_z + dw
    o_ga = o_ba + 4 * DN_HEADS
    o_gb = o_ga + D_MODEL
    main = jnp.concatenate([w[:, :qw], w[:, o_ga:o_gb], w[:, o_gb:], w[:, o_z:o_ba], w[:, o_qkv:o_z]],
                           axis=1)
    pad = jnp.zeros((d, N_SIDE - 2 * kw - (o_ga - o_ba)), w.dtype)
    side = jnp.concatenate([w[:, o_k:o_v], w[:, o_v:o_qkv], w[:, o_ba:o_ga], pad], axis=1)
    return main.astype(BF16), side.astype(BF16)


def _lane_consts(p):
    out = jnp.zeros((DN_HEADS, 1, LANES), F32)
    out = out.at[:, 0, 2].set(p[0])
    return out.at[:, 0, 3].set(p[1])


def kernel(x_prompt, x_sample, c, cache_k_l0, cache_v_l0, state_delta_l0, cache_k_l1, cache_v_l1,
           state_delta_l1, c_ctx, w_ada, b_ada, norm_g, w_in, attn_sink, conv_w, dn_a_log, dn_dt_bias,
           dn_norm_g, w_pa, w_pb, w_o, w_ff_gate, w_ff_up, w_ff_down, w_router, w_e_gate, w_e_up,
           w_e_down):
    Bc, Tc, _ = x_prompt.shape
    Bl, Tl, _ = x_sample.shape
    n_ctx = Bc * Tc
    n_tok = n_ctx + Bl * Tl
    depth = w_in.shape[0]
    caches = ((cache_k_l0, cache_v_l0, state_delta_l0), (cache_k_l1, cache_v_l1, state_delta_l1))

    x_parts = (x_prompt.reshape(n_ctx, D_MODEL), x_sample.reshape(Bl * Tl, D_MODEL), 0)
    cond =jnp.zeros((SUBLANES, D_MODEL), F32).at[:Bl].set(c).at[Bl].set(c_ctx)
    mod = ada_mod(cond, w_ada, b_ada)
    tile_row = np.concatenate([np.full(n_ctx // ROW_TILE, Bl),
                               np.repeat(np.arange(Bl), Tl // ROW_TILE)]).astype(np.int32)
    modt = mod[:, tile_row].reshape(depth, n_tok // ROW_TILE, 6, D_MODEL)
    cos, sin = rope_tables(Tl)

    outs = []
    h = norm_mod(x_parts[0], x_parts[1], norm_g[0, 0:1], modt[0])
    for l in range(depth):
        w_main, w_side = _pack_w_in(w_in[l])
        proj = matmul(h, w_main, BF16)
        side = matmul(h, w_side, F32)
        k_c, v_c, s_c = caches[l]
        oa_c = ctx_attention(proj, side, attn_sink[l], Bc, Tc)
        oa_l = lat_attention(proj, side, attn_sink[l], k_c, v_c, cos, sin, n_ctx, Bl, Tl)
        alog = _lane_consts(dn_a_log[l])
        dtb = _lane_consts(dn_dt_bias[l])
        gdn = dn_norm_g[l].reshape(1, DN_D)
        hps_c = min(DN_HEADS, DN_CHAINS // (2 * (Tc // DN_CHUNK)))
        hps_l = min(DN_HEADS, DN_CHAINS // (2 * (Tl // DN_CHUNK)))
        ob_c, s_fin = deltanet(proj, side, conv_w[l], alog, dtb, gdn, None, 0, Bc, Tc, hps_c)
        ob_l = deltanet(proj, side, conv_w[l], alog, dtb, gdn, s_c, n_ctx, Bl, Tl, hps_l)
        moe = l % 2 == 1
        i = l // 2
        wr = None
        if moe:
            wr = jnp.zeros((D_MODEL, LANES), F32).at[:, :N_EXPERTS].set(w_router[i])
        res = mix(oa_c, oa_l, ob_c, ob_l, proj, *x_parts, modt[l], norm_g[l], w_pa[l].astype(BF16),
                  w_pb[l].astype(BF16), w_o[l].astype(BF16), wr)
        if moe:
            x1, h2, route = res
            pos, n_rows, te, tv, n_used = route_plan(route, MOE_TM)
            xs = dispatch_rows(h2, pos, n_rows)
            ys = moe_grouped(xs, te, tv, n_used, w_e_gate[i], w_e_up[i], w_e_down[i], 512)
            yg = gather_rows(ys, pos)
            x = moe_combine(yg, route, x1, modt[l], norm_g[l])
            h = None if l + 1 == depth else norm_mod(x[:n_ctx], x[n_ctx:], norm_g[l + 1, 0:1], modt[l + 1])
        else:
            x1, h2 = res
            nxt = l + 1 < depth
            r = ffn_dense(h2, w_ff_gate[i].astype(BF16), w_ff_up[i].astype(BF16),
                          w_ff_down[i].astype(BF16), x1, modt[l], norm_g[l],
                          modt[l + 1] if nxt else None, norm_g[l + 1] if nxt else None, 1408)
            x, h = (r[0], r[1]) if nxt else (r[0], None)
        x_parts = (x, x, n_ctx // ROW_TILE)
        k_out = side[:n_ctx, C_K:C_K + N_KV_HEADS * HEAD_DIM].reshape(Bc, Tc, N_KV_HEADS, HEAD_DIM)
        v_out = side[:n_ctx, C_V:C_V + N_KV_HEADS * HEAD_DIM].reshape(Bc, Tc, N_KV_HEADS, HEAD_DIM)
        outs.append((k_out, v_out, s_fin))

    y_prompt = x[:n_ctx].reshape(Bc, Tc, D_MODEL)
    y_sample = x[n_ctx:].reshape(Bl, Tl, D_MODEL)
    return (y_prompt, y_sample, outs[0][0], outs[0][1], outs[0][2], outs[1][0], outs[1][1], outs[1][2])
```

```python
import functools

import numpy as np
import jax
import jax.numpy as jnp
from jax import lax
from jax.experimental import pallas as pl
from jax.experimental.pallas import tpu as pltpu
from jax.experimental.pallas import tpu_sc as plsc

F32 = jnp.float32
BF16 = jnp.bfloat16

D_MODEL = 1024
HEAD_DIM = 128
N_Q_HEADS = 8
N_KV_HEADS = 2
GQA = N_Q_HEADS // N_KV_HEADS
WINDOW = 128
GRID_W = 64
ROPE_BASE = 10000.0
DN_HEADS = 8
DN_D = 128
DN_CONV = 5
DN_CHUNK = 256
DN_LEVELS = (16, 64)
DN_GROUP = 8
DN_CHAINS = 16
N_EXPERTS = 8
TOP_K = 2
EPS = 1e-6
NEG_INF = -1e30

LANES = 128
SUBLANES = 8

ROW_TILE = 256
PROJ_TILE = 1024
FFN_TM = 512
FFN_TF = 1408
MOE_TF = 512
MOE_TM = 1024
MOE_SUB = 512
SC_WINDOW = 32
SC_CORES = 2
SC_SUBCORES = 16

C_Q = 0
C_GA = 1024
C_GB = 2048
C_Z = 3072
C_QB = 4096
C_KB = 5120
C_VB = 6144
N_MAIN = 7168
C_K = 0
C_V = 256
C_BA = 512
N_SIDE = 1024

VMEM_LIMIT = 56 * 1024 * 1024


def _params(*sem):
    return pltpu.CompilerParams(dimension_semantics=sem, vmem_limit_bytes=VMEM_LIMIT)


def _rms(x, g):
    return x * lax.rsqrt(jnp.mean(x * x, axis=-1, keepdims=True) + EPS) * g


def _silu(x):
    return x * jax.nn.sigmoid(x)


def _bdot(a, b):
    return jnp.dot(a.astype(BF16), b.astype(BF16), preferred_element_type=F32)


def _bdot_nt(a, b):
    return lax.dot_general(a.astype(BF16), b.astype(BF16), (((1,), (1,)), ((), ())),
                           preferred_element_type=F32)


def _pieces(x):
    p1 = x.astype(BF16)
    r1 = x - p1.astype(F32)
    p2 = r1.astype(BF16)
    return p1, p2, (r1 - p2.astype(F32)).astype(BF16)


def _dot_split_rhs(a, b):
    dot = functools.partial(jnp.dot, preferred_element_type=F32)
    b1, b2, b3 = _pieces(b)
    return dot(a, b1) + dot(a, b2) + dot(a, b3)


def _dot_split_both(a, b):
    dot = functools.partial(jnp.dot, preferred_element_type=F32)
    ah = a.astype(BF16)
    al = (a - ah.astype(F32)).astype(BF16)
    bh = b.astype(BF16)
    bl = (b - bh.astype(F32)).astype(BF16)
    return dot(ah, bh) + (dot(ah, bl) + dot(al, bh))


def _dot_split_lhs(a, b):
    dot = functools.partial(jnp.dot, preferred_element_type=F32)
    a1, a2, a3 = _pieces(a)
    return dot(a1, b) + dot(a2, b) + dot(a3, b)


def _ada_kernel(c_ref, w_ref, b_ref, o_ref):
    o_ref[0] = _dot_split_both(_silu(c_ref[...]), w_ref[0]) + b_ref[0]


def ada_mod(cond, w_ada, b_ada):
    L = w_ada.shape[0]
    n = w_ada.shape[2] // D_MODEL
    return pl.pallas_call(
        _ada_kernel,
        grid=(L, n),
        in_specs=[pl.BlockSpec((SUBLANES, D_MODEL), lambda l, j: (0, 0)),
                  pl.BlockSpec((1, D_MODEL, D_MODEL), lambda l, j: (l, 0, j)),
                  pl.BlockSpec((1, 1, D_MODEL), lambda l, j: (l, 0, j))],
        out_specs=pl.BlockSpec((1, SUBLANES, D_MODEL), lambda l, j: (l, 0, j)),
        out_shape=jax.ShapeDtypeStruct((L, SUBLANES, 6 * D_MODEL), F32),
        compiler_params=_params("parallel", "parallel"),
        name="ada_mod",
    )(cond, w_ada, b_ada.reshape(L, 1, 6 * D_MODEL))


def _two_part_specs(n_first, off_first, off_second):
    first = pl.BlockSpec((ROW_TILE, D_MODEL), lambda i: (jnp.minimum(i, n_first - 1) + off_first, 0))
    second = pl.BlockSpec((ROW_TILE, D_MODEL), lambda i: (jnp.maximum(i - n_first, 0) + off_second, 0))
    return first, second


def _norm_mod_kernel(n_first, xa_ref, xb_ref, g_ref, mod_ref, h_ref):
    x = jnp.where(pl.program_id(0) < n_first, xa_ref[...], xb_ref[...])
    h = _rms(x, g_ref[...]) * (1.0 + mod_ref[0, 1:2, :]) + mod_ref[0, 0:1, :]
    h_ref[...] = h.astype(h_ref.dtype)


def norm_mod(xa, xb, g, modt):
    na = xa.shape[0] // ROW_TILE
    n = na + xb.shape[0] // ROW_TILE
    return pl.pallas_call(
        functools.partial(_norm_mod_kernel, na),
        grid=(n,),
        in_specs=[*_two_part_specs(na, 0, 0),
                  pl.BlockSpec((1, D_MODEL), lambda i: (0, 0)),
                  pl.BlockSpec((1, 6, D_MODEL), lambda i: (i, 0, 0))],
        out_specs=pl.BlockSpec((ROW_TILE, D_MODEL), lambda i: (i, 0)),
        out_shape=jax.ShapeDtypeStruct((n * ROW_TILE, D_MODEL), BF16),
        compiler_params=_params("parallel"),
        name="norm_mod",
    )(xa, xb, g, modt)


def _mm_kernel(a_ref, b_ref, o_ref):
    o_ref[...] = jnp.dot(a_ref[...], b_ref[...], preferred_element_type=F32).astype(o_ref.dtype)


def matmul(a, b, out_dtype):
    M, K = a.shape
    N = b.shape[1]
    tm = tn = PROJ_TILE
    return pl.pallas_call(
        _mm_kernel,
        grid=(N // tn, M // tm),
        in_specs=[pl.BlockSpec((tm, K), lambda j, i: (i, 0)),
                  pl.BlockSpec((K, tn), lambda j, i: (0, j))],
        out_specs=pl.BlockSpec((tm, tn), lambda j, i: (i, j)),
        out_shape=jax.ShapeDtypeStruct((M, N), out_dtype),
        compiler_params=_params("parallel", "parallel"),
        name="in_proj",
    )(a, b)


def _side_kernel(n_ctx_tiles, seqs, T, a_ref, b_ref, o_ref, k_ref, v_ref):
    r = jnp.dot(a_ref[...], b_ref[...], preferred_element_type=F32)
    o_ref[...] = r

    @pl.when(pl.program_id(0) < n_ctx_tiles)
    def _():
        kw = N_KV_HEADS * HEAD_DIM
        k_ref[...] = r[:, C_K:C_K + kw].reshape(seqs, T, N_KV_HEADS, HEAD_DIM)
        v_ref[...] = r[:, C_V:C_V + kw].reshape(seqs, T, N_KV_HEADS, HEAD_DIM)


def side_proj(a, b, n_seq, T):
    M, K = a.shape
    tm = PROJ_TILE
    seqs = tm // T
    n_ctx_tiles = n_seq // seqs
    assert seqs * T == tm and n_ctx_tiles * seqs == n_seq
    kv_spec = pl.BlockSpec((seqs, T, N_KV_HEADS, HEAD_DIM),
                           lambda i: (jnp.minimum(i, n_ctx_tiles - 1), 0, 0, 0))
    kv_shape = jax.ShapeDtypeStruct((n_seq, T, N_KV_HEADS, HEAD_DIM), F32)
    return pl.pallas_call(
        functools.partial(_side_kernel, n_ctx_tiles, seqs, T),
        grid=(M // tm,),
        in_specs=[pl.BlockSpec((tm, K), lambda i: (i, 0)),
                  pl.BlockSpec((K, N_SIDE), lambda i: (0, 0))],
        out_specs=(pl.BlockSpec((tm, N_SIDE), lambda i: (i, 0)), kv_spec, kv_spec),
        out_shape=(jax.ShapeDtypeStruct((M, N_SIDE), F32), kv_shape, kv_shape),
        compiler_params=_params("arbitrary"),
        name="side_proj",
    )(a, b)


def _ctx_attn_kernel(sink_ref, q_ref, k_ref, v_ref, o_ref):
    hk = pl.program_id(1)
    k = k_ref[...].astype(BF16)
    v = v_ref[...].astype(BF16)
    scale = HEAD_DIM ** -0.5
    for g in range(GQA):
        q = q_ref[:, g * HEAD_DIM:(g + 1) * HEAD_DIM]
        s = _bdot_nt(q, k) * scale
        sink = sink_ref[hk * GQA + g]
        m = jnp.maximum(jnp.max(s, axis=-1, keepdims=True), sink)
        p = jnp.exp(s - m)
        den = jnp.sum(p, axis=-1, keepdims=True) + jnp.exp(sink - m)
        o = _bdot(p, v) / den
        o_ref[:, g * HEAD_DIM:(g + 1) * HEAD_DIM] = o.astype(o_ref.dtype)


def ctx_attention(proj, side, sink, n_seq, T):
    qw = GQA * HEAD_DIM
    return pl.pallas_call(
        _ctx_attn_kernel,
        grid_spec=pltpu.PrefetchScalarGridSpec(
            num_scalar_prefetch=1,
            grid=(n_seq, N_KV_HEADS),
            in_specs=[pl.BlockSpec((T, qw), lambda b, h, s: (b, C_Q // qw + h)),
                      pl.BlockSpec((T, HEAD_DIM), lambda b, h, s: (b, C_K // HEAD_DIM + h)),
                      pl.BlockSpec((T, HEAD_DIM), lambda b, h, s: (b, C_V // HEAD_DIM + h))],
            out_specs=pl.BlockSpec((T, qw), lambda b, h, s: (b, h))),
        out_shape=jax.ShapeDtypeStruct((n_seq * T, N_Q_HEADS * HEAD_DIM), BF16),
        compiler_params=_params("parallel", "parallel"),
        name="ctx_attn",
    )(sink, proj, side, side)


def _rope(x, cos, sin):
    lane = lax.broadcasted_iota(jnp.int32, x.shape, 1)
    q = HEAD_DIM // 4
    swapped = jnp.where((lane % (2 * q)) < q,
                        pltpu.roll(x, HEAD_DIM - q, 1), pltpu.roll(x, q, 1))
    return x * cos + swapped * sin


def _lat_attn_kernel(T, sink_ref, q_ref, k_ref, v_ref, kc_ref, vc_ref, cosq_ref, sinq_ref,
                     cos_ref, sin_ref, o_ref, kpad, vpad):
    hk = pl.program_id(1)
    i = pl.program_id(2)
    B = WINDOW

    @pl.when(i == 0)
    def _():
        zeros = jnp.zeros((B, HEAD_DIM), BF16)
        kpad[0:B, :] = zeros
        kpad[T + B:T + 2 * B, :] = zeros
        vpad[0:B, :] = zeros
        vpad[T + B:T + 2 * B, :] = zeros
        kpad[B:T + B, :] = _rope(k_ref[...], cos_ref[...], sin_ref[...]).astype(BF16)
        vpad[B:T + B, :] = v_ref[...].astype(BF16)

    start = pl.multiple_of(i * B, B)
    kw = kpad[pl.ds(start, 3 * B), :]
    vw = vpad[pl.ds(start, 3 * B), :]
    kc = kc_ref[0].astype(BF16)
    vc = vc_ref[0].astype(BF16)

    rows = GQA * B
    qi = lax.broadcasted_iota(jnp.int32, (rows, B), 0) % B
    kj = lax.broadcasted_iota(jnp.int32, (rows, B), 1)
    mask_prev = (kj >= qi) & (i > 0)
    mask_next = (kj <= qi) & (i < pl.num_programs(2) - 1)

    cq = cosq_ref[...]
    sq = sinq_ref[...]
    scale = HEAD_DIM ** -0.5
    q = jnp.concatenate(
        [_rope(q_ref[:, g * HEAD_DIM:(g + 1) * HEAD_DIM].astype(F32), cq, sq) for g in range(GQA)],
        axis=0) * scale
    s_c = _bdot_nt(q, kc)
    s_w = _bdot_nt(q, kw)
    s_w = jnp.concatenate([jnp.where(mask_prev, s_w[:, 0:B], NEG_INF), s_w[:, B:2 * B],
                           jnp.where(mask_next, s_w[:, 2 * B:3 * B], NEG_INF)], axis=1)
    gi = lax.broadcasted_iota(jnp.int32, (rows, 1), 0) // B
    sink = jnp.zeros((rows, 1), F32)
    for g in range(GQA):
        sink = jnp.where(gi == g, sink_ref[hk * GQA + g], sink)
    m = jnp.maximum(jnp.maximum(jnp.max(s_c, axis=-1, keepdims=True),
                                jnp.max(s_w, axis=-1, keepdims=True)), sink)
    p_c = jnp.exp(s_c - m)
    p_w = jnp.exp(s_w - m)
    den = (jnp.sum(p_c, axis=-1, keepdims=True) + jnp.sum(p_w, axis=-1, keepdims=True)
           + jnp.exp(sink - m))
    o = (_bdot(p_c, vc) + _bdot(p_w, vw)) / den
    for g in range(GQA):
        o_ref[:, g * HEAD_DIM:(g + 1) * HEAD_DIM] = o[g * B:(g + 1) * B, :].astype(o_ref.dtype)


def lat_attention(proj, side, sink, cache_k, cache_v, cos, sin, row0, n_seq, T):
    qw = GQA * HEAD_DIM
    B = WINDOW
    nb = T // B
    P = cache_k.shape[1]
    kc = cache_k.reshape(n_seq, P, N_KV_HEADS * HEAD_DIM)
    vc = cache_v.reshape(n_seq, P, N_KV_HEADS * HEAD_DIM)
    rb = row0 // B
    rt = row0 // T
    return pl.pallas_call(
        functools.partial(_lat_attn_kernel, T),
        grid_spec=pltpu.PrefetchScalarGridSpec(
            num_scalar_prefetch=1,
            grid=(n_seq, N_KV_HEADS, nb),
            in_specs=[pl.BlockSpec((B, qw), lambda b, h, i, s: (rb + b * nb + i, C_Q // qw + h)),
                      pl.BlockSpec((T, HEAD_DIM), lambda b, h, i, s: (rt + b, C_K // HEAD_DIM + h)),
                      pl.BlockSpec((T, HEAD_DIM), lambda b, h, i, s: (rt + b, C_V // HEAD_DIM + h)),
                      pl.BlockSpec((1, P, HEAD_DIM), lambda b, h, i, s: (b, 0, h)),
                      pl.BlockSpec((1, P, HEAD_DIM), lambda b, h, i, s: (b, 0, h)),
                      pl.BlockSpec((B, HEAD_DIM), lambda b, h, i, s: (i, 0)),
                      pl.BlockSpec((B, HEAD_DIM), lambda b, h, i, s: (i, 0)),
                      pl.BlockSpec((T, HEAD_DIM), lambda b, h, i, s: (0, 0)),
                      pl.BlockSpec((T, HEAD_DIM), lambda b, h, i, s: (0, 0))],
            out_specs=pl.BlockSpec((B, qw), lambda b, h, i, s: (b * nb + i, h)),
            scratch_shapes=[pltpu.VMEM((T + 2 * B, HEAD_DIM), BF16),
                            pltpu.VMEM((T + 2 * B, HEAD_DIM), BF16)]),
        out_shape=jax.ShapeDtypeStruct((n_seq * T, N_Q_HEADS * HEAD_DIM), BF16),
        compiler_params=_params("parallel", "parallel", "arbitrary"),
        name="lat_attn",
    )(sink, proj, side, side, kc, vc, cos, sin, cos, sin)


def rope_tables(T):
    rows = (np.arange(T) // GRID_W).astype(np.float32)
    cols = (np.arange(T) % GRID_W).astype(np.float32)
    n_freq = HEAD_DIM // 4
    inv = jnp.asarray(ROPE_BASE, F32) ** (-jnp.arange(n_freq, dtype=F32) / n_freq)
    ar = jnp.asarray(rows)[:, None] * inv
    ac = jnp.asarray(cols)[:, None] * inv
    cos = jnp.concatenate([jnp.cos(ar), jnp.cos(ar), jnp.cos(ac), jnp.cos(ac)], axis=-1)
    sin = jnp.concatenate([-jnp.sin(ar), jnp.sin(ar), -jnp.sin(ac), jnp.sin(ac)], axis=-1)
    return cos, sin


def _dn_kernel(T, hps, has_s0, *refs):
    n_in = 12 if has_s0 else 11
    (q_ref, k_ref, v_ref, z_ref, ba_ref, wq_ref, wk_ref, wv_ref, alog_ref, dt_ref, gn_ref) = refs[:11]
    s0_ref = refs[11] if has_s0 else None
    if has_s0:
        o_ref = refs[n_in]
        sfin_ref = None
        scratch = refs[n_in + 1:]
    else:
        o_ref, sfin_ref = refs[n_in], refs[n_in + 1]
        scratch = refs[n_in + 2:]
    xpad, qs, ks, vs, cols, gcs, gcT, kTs, U, WQ, QKM, KDT, EGL, OST = scratch

    hg = pl.program_id(1)
    C = DN_CHUNK
    D = DN_D
    n = T // C
    half = DN_CONV // 2
    PAD = SUBLANES

    zpad = jnp.zeros((PAD, D), F32)
    xpad[0:PAD, :] = zpad
    xpad[T + PAD:T + 2 * PAD, :] = zpad

    def conv_silu(x_ref, w_ref, hh):
        xpad[PAD:T + PAD, :] = x_ref[:, hh * D:(hh + 1) * D].astype(F32)
        w = w_ref[:, hh * D:(hh + 1) * D]
        acc = xpad[pl.ds(PAD - half, T), :] * w[0:1, :]
        for j in range(1, DN_CONV):
            acc = acc + xpad[pl.ds(PAD - half + j, T), :] * w[j:j + 1, :]
        return _silu(acc)

    def l2n(x):
        return x * lax.rsqrt(jnp.sum(x * x, axis=-1, keepdims=True) + EPS)

    ri = lax.broadcasted_iota(jnp.int32, (C, C), 0)
    ci = lax.broadcasted_iota(jnp.int32, (C, C), 1)
    eye = jnp.where(ri == ci, 1.0, 0.0).astype(F32)
    level_mask = []
    inside_prev = None
    for size in DN_LEVELS + (C,):
        inside = (ri // size) == (ci // size)
        joins = inside if inside_prev is None else inside & jnp.logical_not(inside_prev)
        level_mask.append(jnp.where(joins, 1.0, 0.0).astype(BF16))
        inside_prev = inside
    lower = jnp.where(ri >= ci, 1.0, 0.0).astype(BF16)
    li = lax.broadcasted_iota(jnp.int32, (LANES, LANES), 0)
    lj = lax.broadcasted_iota(jnp.int32, (LANES, LANES), 1)
    lane_t = lax.broadcasted_iota(jnp.int32, (T, LANES), 1)
    lane_c = lax.broadcasted_iota(jnp.int32, (C, LANES), 1)

    def prologue(hh):
        def conv_q():
            qs[hh] = l2n(conv_silu(q_ref, wq_ref, hh)) * (D ** -0.5)

        def conv_k():
            ks[hh] = l2n(conv_silu(k_ref, wk_ref, hh))

        def conv_v():
            vs[hh] = conv_silu(v_ref, wv_ref, hh)

        def gates():
            head = hg * hps + hh
            sel = jnp.where((lj < 4) & (li == lj * DN_HEADS + head), 1.0, 0.0).astype(BF16)
            raw = _dot_split_lhs(ba_ref[...], sel)
            gate = -jnp.exp(alog_ref[hh]) * jax.nn.softplus(raw + dt_ref[hh])
            cols[hh] = jnp.where(lane_t < 2, jax.nn.sigmoid(raw), gate)

        def cumulative(c):
            rows = slice(c * C, (c + 1) * C)
            g = cols[hh, rows, :]
            pre = _dot_split_rhs(lower, g)
            gc = jnp.where(lane_c == 3, pre[C - 1:C, :] - pre + g, pre)
            gcs[hh, rows, :] = gc
            gcT[hh * n + c] = gc.T[0:SUBLANES, :]
            kTs[hh * n + c] = ks[hh, rows, :].T

        return [conv_q, conv_k, conv_v, gates] + [functools.partial(cumulative, c) for c in range(n)]

    chains = [(hh, c, d) for hh in range(hps) for c in range(n) for d in (0, 1)]
    chain_id = {ch: j for j, ch in enumerate(chains)}

    def nilpotent_inverse(ms, squarings):
        dot = functools.partial(jnp.dot, preferred_element_type=F32)
        ts = [eye + m.astype(F32) for m in ms]
        pws = ms
        for _ in range(squarings):
            pws = [dot(pw, pw).astype(BF16) for pw in pws]
            ts = [t + dot(t.astype(BF16), pw) for t, pw in zip(ts, pws)]
        return ts

    def phase_a(j0, grp):
        G = range(len(grp))
        rows = [slice(c * C, (c + 1) * C) for (_, c, _) in grp]
        incl = [(ri >= ci) if d == 0 else (ri <= ci) for (_, _, d) in grp]
        strict = [(ri > ci) if d == 0 else (ri < ci) for (_, _, d) in grp]
        q = [qs[grp[i][0], rows[i], :] for i in G]
        k = [ks[grp[i][0], rows[i], :] for i in G]
        v = [vs[grp[i][0], rows[i], :] for i in G]
        gcc = [gcs[hh, rows[i], 2 + d:3 + d] for i, (hh, _, d) in enumerate(grp)]
        bcol = [cols[hh, rows[i], d:d + 1] for i, (hh, _, d) in enumerate(grp)]
        kT = [kTs[hh * n + c] for (hh, c, _) in grp]
        gcr = [gcT[hh * n + c, 2 + d:3 + d, :] for (hh, c, d) in grp]
        g_last = [gcr[i][:, C - 1:C] if grp[i][2] == 0 else gcr[i][:, 0:1] for i in G]
        decay = [jnp.where(incl[i], jnp.exp(jnp.where(incl[i], gcc[i] - gcr[i], 0.0)), 0.0) for i in G]
        qkk_of = {}
        for i in G:
            if grp[i][:2] not in qkk_of:
                qkk_of[grp[i][:2]] = _bdot(jnp.concatenate([q[i], k[i]], axis=0), kT[i])
        qkk = [qkk_of[grp[i][:2]] for i in G]
        a_neg = [jnp.where(strict[i], -(qkk[i][C:2 * C, :] * bcol[i] * decay[i]), 0.0).astype(BF16)
                 for i in G]
        sizes = DN_LEVELS + (C,)
        t_inv = nilpotent_inverse([a_neg[i] * level_mask[0] for i in G], sizes[0].bit_length() - 2)
        for lvl in range(1, len(sizes)):
            e_neg = [_bdot(t_inv[i], a_neg[i] * level_mask[lvl]).astype(BF16) for i in G]
            f = nilpotent_inverse(e_neg, (sizes[lvl] // sizes[lvl - 1]).bit_length() - 2)
            t_inv = [_bdot(f[i], t_inv[i]) for i in G]
        egc = [jnp.exp(gcc[i]) for i in G]
        uw = [_bdot(t_inv[i], jnp.concatenate([v[i] * bcol[i], k[i] * (bcol[i] * egc[i])], axis=1))
              for i in G]
        r0, r1 = j0, j0 + len(grp)
        U[r0 * C:r1 * C, :] = jnp.concatenate([uw[i][:, 0:D] for i in G], axis=0)
        WQ[r0 * 2 * C:r1 * 2 * C, :] = jnp.concatenate(
            [jnp.concatenate([uw[i][:, D:2 * D], q[i] * egc[i]], axis=0) for i in G], axis=0).astype(BF16)
        QKM[r0 * C:r1 * C, :] = jnp.concatenate(
            [qkk[i][0:C, :] * decay[i] for i in G], axis=0).astype(BF16)
        KDT[r0 * D:r1 * D, :] = jnp.concatenate(
            [kT[i] * jnp.exp(g_last[i] - gcr[i]) for i in G], axis=0).astype(BF16)
        EGL[r0 * SUBLANES:r1 * SUBLANES, :] = jnp.concatenate(
            [jnp.broadcast_to(jnp.exp(g_last[i]), (SUBLANES, LANES)) for i in G], axis=0)

    for hh in range(hps):
        for step in prologue(hh):
            step()
    for j0 in range(0, len(chains), DN_GROUP):
        phase_a(j0, chains[j0:j0 + DN_GROUP])

    if has_s0:
        state = [s0_ref[0, d, hh] for hh in range(hps) for d in (0, 1)]
    else:
        state = [jnp.zeros((D, D), F32) for _ in range(2 * hps)]
    for i in range(n):
        outs = []
        new_state = []
        for hh in range(hps):
            for d in (0, 1):
                j = chain_id[(hh, i if d == 0 else n - 1 - i, d)]
                s_old = state[hh * 2 + d]
                xs = jnp.dot(WQ[j * 2 * C:(j + 1) * 2 * C, :], s_old.astype(BF16),
                             preferred_element_type=F32)
                v_new = (U[j * C:(j + 1) * C, :] - xs[0:C, :]).astype(BF16)
                outs.append(xs[C:2 * C, :] + jnp.dot(QKM[j * C:(j + 1) * C, :], v_new,
                                                     preferred_element_type=F32))
                new_state.append(s_old * EGL[j * SUBLANES:j * SUBLANES + 1, 0:1]
                                 + jnp.dot(KDT[j * D:(j + 1) * D, :], v_new, preferred_element_type=F32))
        OST[i] = jnp.concatenate(outs, axis=0)
        state = new_state

    if not has_s0:
        for hh in range(hps):
            for d in (0, 1):
                sfin_ref[0, d, hh] = state[hh * 2 + d]

    for hh in range(hps):
        for c in range(n):
            rows = slice(c * C, (c + 1) * C)
            o = (OST[c, (2 * hh) * C:(2 * hh + 1) * C, :]
                 + OST[n - 1 - c, (2 * hh + 1) * C:(2 * hh + 2) * C, :])
            y = _rms(o, gn_ref[...]) * _silu(z_ref[rows, hh * D:(hh + 1) * D].astype(F32))
            o_ref[rows, hh * D:(hh + 1) * D] = y.astype(o_ref.dtype)


def deltanet(proj, side, conv_w, alog_l, dt_l, gn, s0, row0, n_seq, T, hps):
    rt = row0 // T
    has_s0 = s0 is not None
    W = hps * DN_D
    n = T // DN_CHUNK
    n_chain = hps * n * 2

    def col(c0):
        return pl.BlockSpec((T, W), lambda b, h: (rt + b, c0 // W + h))

    def wcol(c0):
        return pl.BlockSpec((DN_CONV, W), lambda b, h: (0, c0 // W + h))

    in_specs = [col(C_QB), col(C_KB), col(C_VB), col(C_Z),
                pl.BlockSpec((T, LANES), lambda b, h: (rt + b, C_BA // LANES)),
                wcol(0), wcol(DN_HEADS * DN_D), wcol(2 * DN_HEADS * DN_D),
                pl.BlockSpec((hps, 1, LANES), lambda b, h: (h, 0, 0)),
                pl.BlockSpec((hps, 1, LANES), lambda b, h: (h, 0, 0)),
                pl.BlockSpec((1, DN_D), lambda b, h: (0, 0))]
    args = [proj, proj, proj, proj, side, conv_w, conv_w, conv_w, alog_l, dt_l, gn]
    o_spec = pl.BlockSpec((T, W), lambda b, h: (b, h))
    o_shape = jax.ShapeDtypeStruct((n_seq * T, DN_HEADS * DN_D), BF16)
    s_spec = pl.BlockSpec((1, 2, hps, DN_D, DN_D), lambda b, h: (b, 0, h, 0, 0))
    if has_s0:
        in_specs.append(s_spec)
        args.append(s0)
        out_specs, out_shape = o_spec, o_shape
    else:
        out_specs = (o_spec, s_spec)
        out_shape = (o_shape, jax.ShapeDtypeStruct((n_seq, 2, DN_HEADS, DN_D, DN_D), F32))
    C = DN_CHUNK
    scratch = [pltpu.VMEM((T + 2 * SUBLANES, DN_D), F32),
               pltpu.VMEM((hps, T, DN_D), F32),
               pltpu.VMEM((hps, T, DN_D), F32),
               pltpu.VMEM((hps, T, DN_D), F32),
               pltpu.VMEM((hps, T, LANES), F32),
               pltpu.VMEM((hps, T, LANES), F32),
               pltpu.VMEM((hps * n, SUBLANES, C), F32),
               pltpu.VMEM((hps * n, DN_D, C), F32),
               pltpu.VMEM((n_chain * C, DN_D), F32),
               pltpu.VMEM((n_chain * 2 * C, DN_D), BF16),
               pltpu.VMEM((n_chain * C, C), BF16),
               pltpu.VMEM((n_chain * DN_D, C), BF16),
               pltpu.VMEM((n_chain * SUBLANES, LANES), F32),
               pltpu.VMEM((n, 2 * hps * C, DN_D), F32)]
    return pl.pallas_call(
        functools.partial(_dn_kernel, T, hps, has_s0),
        grid=(n_seq, DN_HEADS // hps),
        in_specs=in_specs,
        out_specs=out_specs,
        out_shape=out_shape,
        scratch_shapes=scratch,
        compiler_params=_params("parallel", "parallel"),
        name="deltanet_lat" if has_s0 else "deltanet_ctx",
    )(*args)


def _route_top2(logits):
    lane = lax.broadcasted_iota(jnp.int32, logits.shape, 1).astype(F32)
    lg = jnp.where(lane < N_EXPERTS, logits, -jnp.inf)
    m1 = jnp.max(lg, axis=-1, keepdims=True)
    i1 = jnp.min(jnp.where(lg == m1, lane, float(LANES)), axis=-1, keepdims=True)
    lg2 = jnp.where(lane == i1, -jnp.inf, lg)
    m2 = jnp.max(lg2, axis=-1, keepdims=True)
    i2 = jnp.min(jnp.where(lg2 == m2, lane, float(LANES)), axis=-1, keepdims=True)
    e2 = jnp.exp(m2 - m1)
    den = 1.0 + e2
    return jnp.where(lane == 0.0, i1, jnp.where(lane == 1.0, i2,
                     jnp.where(lane == 2.0, 1.0 / den, jnp.where(lane == 3.0, e2 / den, 0.0))))


def _mix_kernel(route, n_ctx_tiles, oac_ref, oal_ref, obc_ref, obl_ref, ga_ref, gb_ref, xc_ref,
                xl_ref, mod_ref, gn_ref, wpa_ref, wpb_ref, wo_ref, *rest):
    if route:
        wr_ref, x1_ref, h2_ref, rt_ref = rest
    else:
        x1_ref, h2_ref = rest
    is_ctx = pl.program_id(0) < n_ctx_tiles
    o_a = jnp.where(is_ctx, oac_ref[...], oal_ref[...])
    o_b = jnp.where(is_ctx, obc_ref[...], obl_ref[...])
    x = jnp.where(is_ctx, xc_ref[...], xl_ref[...])
    br_a = jnp.dot(o_a, wpa_ref[...], preferred_element_type=F32)
    br_b = jnp.dot(o_b, wpb_ref[...], preferred_element_type=F32)
    merged = (jax.nn.sigmoid(ga_ref[...].astype(F32)) * br_a
              + jax.nn.sigmoid(gb_ref[...].astype(F32)) * br_b)
    y = jnp.dot(merged.astype(BF16), wo_ref[...], preferred_element_type=F32)
    x1 = x + mod_ref[0, 2:3, :] * _rms(y, gn_ref[1:2, :])
    x1_ref[...] = x1
    h2 = _rms(x1, gn_ref[2:3, :]) * (1.0 + mod_ref[0, 4:5, :]) + mod_ref[0, 3:4, :]
    h2_ref[...] = h2.astype(h2_ref.dtype)
    if route:
        rt_ref[...] = _route_top2(_dot_split_both(h2, wr_ref[...]))


def mix(oa_c, oa_l, ob_c, ob_l, proj, x_c, x_l, x_l_tile0, modt, gn, wpa, wpb, wo, w_router):
    nc = oa_c.shape[0] // ROW_TILE
    nl = oa_l.shape[0] // ROW_TILE
    n = nc + nl
    n_tok = n * ROW_TILE
    route = w_router is not None
    row = lambda i: (i, 0)
    full = lambda i: (0, 0)
    crow, lrow = _two_part_specs(nc, 0, 0)
    in_specs = [crow, lrow, crow, lrow,
                pl.BlockSpec((ROW_TILE, D_MODEL), lambda i: (i, C_GA // D_MODEL)),
                pl.BlockSpec((ROW_TILE, D_MODEL), lambda i: (i, C_GB // D_MODEL)),
                *_two_part_specs(nc, 0, x_l_tile0),
                pl.BlockSpec((1, 6, D_MODEL), lambda i: (i, 0, 0)),
                pl.BlockSpec((4, D_MODEL), full),
                pl.BlockSpec((D_MODEL, D_MODEL), full),
                pl.BlockSpec((D_MODEL, D_MODEL), full),
                pl.BlockSpec((D_MODEL, D_MODEL), full)]
    args = [oa_c, oa_l, ob_c, ob_l, proj, proj, x_c, x_l, modt, gn, wpa, wpb, wo]
    out_specs = [pl.BlockSpec((ROW_TILE, D_MODEL), row), pl.BlockSpec((ROW_TILE, D_MODEL), row)]
    out_shape = [jax.ShapeDtypeStruct((n_tok, D_MODEL), F32),
                 jax.ShapeDtypeStruct((n_tok, D_MODEL), F32 if route else BF16)]
    if route:
        in_specs.append(pl.BlockSpec((D_MODEL, LANES), full))
        args.append(w_router)
        out_specs.append(pl.BlockSpec((ROW_TILE, LANES), row))
        out_shape.append(jax.ShapeDtypeStruct((n_tok, LANES), F32))
    return pl.pallas_call(
        functools.partial(_mix_kernel, route, nc),
        grid=(n,),
        in_specs=in_specs,
        out_specs=tuple(out_specs),
        out_shape=tuple(out_shape),
        compiler_params=_params("parallel"),
        name="mix_route" if route else "mix",
    )(*args)


def _ffn_kernel(next_norm, h_ref, wg_ref, wu_ref, wd_ref, x_ref, mod_ref, gn_ref, *rest):
    if next_norm:
        modn_ref, gnn_ref, x2_ref, hn_ref, acc = rest
    else:
        x2_ref, acc = rest
    f = pl.program_id(1)
    h = h_ref[...]
    gate = jnp.dot(h, wg_ref[...], preferred_element_type=F32)
    up = jnp.dot(h, wu_ref[...], preferred_element_type=F32)
    part = jnp.dot((_silu(gate) * up).astype(BF16), wd_ref[...], preferred_element_type=F32)
    acc[...] = jnp.where(f == 0, 0.0, acc[...]) + part

    @pl.when(f == pl.num_programs(1) - 1)
    def _():
        for r in range(FFN_TM // ROW_TILE):
            rs = slice(r * ROW_TILE, (r + 1) * ROW_TILE)
            x2 = x_ref[rs, :] + mod_ref[r, 5:6, :] * _rms(acc[rs, :], gn_ref[3:4, :])
            x2_ref[rs, :] = x2
            if next_norm:
                hn = _rms(x2, gnn_ref[0:1, :]) * (1.0 + modn_ref[r, 1:2, :]) + modn_ref[r, 0:1, :]
                hn_ref[rs, :] = hn.astype(hn_ref.dtype)


def ffn_dense(h2, wg, wu, wd, x1, modt, gn, modt_next, gn_next, tf):
    n_tok = x1.shape[0]
    F = wg.shape[1]
    nr = FFN_TM // ROW_TILE
    next_norm = modt_next is not None
    row = lambda i, f: (i, 0)
    in_specs = [pl.BlockSpec((FFN_TM, D_MODEL), row),
                pl.BlockSpec((D_MODEL, tf), lambda i, f: (0, f)),
                pl.BlockSpec((D_MODEL, tf), lambda i, f: (0, f)),
                pl.BlockSpec((tf, D_MODEL), lambda i, f: (f, 0)),
                pl.BlockSpec((FFN_TM, D_MODEL), row),
                pl.BlockSpec((nr, 6, D_MODEL), lambda i, f: (i, 0, 0)),
                pl.BlockSpec((4, D_MODEL), lambda i, f: (0, 0))]
    args = [h2, wg, wu, wd, x1, modt, gn]
    out_specs = [pl.BlockSpec((FFN_TM, D_MODEL), row)]
    out_shape = [jax.ShapeDtypeStruct((n_tok, D_MODEL), F32)]
    if next_norm:
        in_specs += [pl.BlockSpec((nr, 6, D_MODEL), lambda i, f: (i, 0, 0)),
                     pl.BlockSpec((4, D_MODEL), lambda i, f: (0, 0))]
        args += [modt_next, gn_next]
        out_specs.append(pl.BlockSpec((FFN_TM, D_MODEL), row))
        out_shape.append(jax.ShapeDtypeStruct((n_tok, D_MODEL), BF16))
    return pl.pallas_call(
        functools.partial(_ffn_kernel, next_norm),
        grid=(n_tok // FFN_TM, F // tf),
        in_specs=in_specs,
        out_specs=tuple(out_specs),
        out_shape=tuple(out_shape),
        scratch_shapes=[pltpu.VMEM((FFN_TM, D_MODEL), F32)],
        compiler_params=_params("parallel", "arbitrary"),
        name="ffn_dense",
    )(*args)


def gather_rows(x, idx):
    n = idx.shape[0]
    w = x.shape[1]
    per_worker = n // (SC_CORES * SC_SUBCORES)
    steps = per_worker // SC_WINDOW
    assert steps * SC_WINDOW * SC_CORES * SC_SUBCORES == n
    mesh = plsc.VectorSubcoreMesh(core_axis_name="core", subcore_axis_name="subcore")

    @functools.partial(pl.kernel, out_type=jax.ShapeDtypeStruct((n, w), x.dtype), mesh=mesh,
                       scratch_types=[pltpu.VMEM((SC_WINDOW,), jnp.int32),
                                      pltpu.VMEM((SC_WINDOW, w), x.dtype),
                                      pltpu.SemaphoreType.DMA])
    def gather_kernel(x_hbm, i_hbm, o_hbm, idx_v, rows_v, sem):
        worker = lax.axis_index("subcore") * SC_CORES + lax.axis_index("core")
        base = worker * per_worker

        @pl.loop(0, steps)
        def _(s):
            off = base + s * SC_WINDOW
            pltpu.sync_copy(i_hbm.at[pl.ds(off, SC_WINDOW)], idx_v)
            pltpu.async_copy(x_hbm.at[idx_v], rows_v, sem).wait()
            pltpu.sync_copy(rows_v, o_hbm.at[pl.ds(off, SC_WINDOW)])

    return gather_kernel(x, idx)


def dispatch_rows(x, pos, n_rows):
    n_tok, w = x.shape
    per_worker = n_tok // (SC_CORES * SC_SUBCORES)
    steps = per_worker // SC_WINDOW
    assert steps * SC_WINDOW * SC_CORES * SC_SUBCORES == n_tok
    mesh = plsc.VectorSubcoreMesh(core_axis_name="core", subcore_axis_name="subcore")

    @functools.partial(pl.kernel, out_type=jax.ShapeDtypeStruct((n_rows, w), x.dtype), mesh=mesh,
                       scratch_types=[pltpu.VMEM((SC_WINDOW,), jnp.int32),
                                      pltpu.VMEM((SC_WINDOW, w), x.dtype)])
    def dispatch_kernel(x_hbm, p_hbm, o_hbm, idx_v, rows_v):
        worker = lax.axis_index("subcore") * SC_CORES + lax.axis_index("core")
        base = worker * per_worker

        @pl.loop(0, steps)
        def _(s):
            off = base + s * SC_WINDOW
            pltpu.sync_copy(x_hbm.at[pl.ds(off, SC_WINDOW)], rows_v)
            for k in range(TOP_K):
                pltpu.sync_copy(p_hbm.at[pl.ds(k * n_tok + off, SC_WINDOW)], idx_v)
                pltpu.sync_copy(rows_v, o_hbm.at[idx_v])

    return dispatch_kernel(x, pos)


def route_plan(route, tm):
    n_tok = route.shape[0]
    ef = route[:, 0:TOP_K].astype(jnp.int32).T.reshape(-1)
    onehot = (ef[:, None] == jnp.arange(N_EXPERTS, dtype=jnp.int32)[None, :]).astype(jnp.int32)
    csum = jnp.cumsum(onehot, axis=0)
    rank = jnp.sum(csum * onehot, axis=1) - 1
    counts = csum[-1]
    padded = ((counts + tm - 1) // tm) * tm
    ends = jnp.cumsum(padded)
    pos = (ends - padded)[ef] + rank
    n_rows = TOP_K * n_tok + N_EXPERTS * tm
    n_tiles = n_rows // tm
    n_used = (ends[-1] // tm).astype(jnp.int32)
    tile_start = jnp.arange(n_tiles, dtype=jnp.int32) * tm
    te = jnp.sum((tile_start[:, None] >= ends[None, :]).astype(jnp.int32), axis=1)
    te = jnp.where(jnp.arange(n_tiles) < n_used, te, te[n_used - 1])
    real_end = ends - padded + counts
    tv = jnp.clip(real_end[te] - tile_start, 0, tm)
    tv = jnp.where(jnp.arange(n_tiles) < n_used, tv, 0)
    return pos.astype(jnp.int32), n_rows, te.astype(jnp.int32), tv.astype(jnp.int32), n_used.reshape(1)


def _moe_kernel(te_ref, tv_ref, nu_ref, x_ref, wg_ref, wu_ref, wd_ref, y_ref, acc):
    i = pl.program_id(0)
    f = pl.program_id(1)
    last = f == pl.num_programs(1) - 1
    used = i < nu_ref[0]
    valid = tv_ref[i]
    n_sub = MOE_TM // MOE_SUB

    @pl.when(used)
    def _():
        wg = wg_ref[0].astype(BF16)
        wu = wu_ref[0].astype(BF16)
        wd = wd_ref[0].astype(BF16)

        def sub_tile(r):
            rows = slice(r * MOE_SUB, (r + 1) * MOE_SUB)
            x = x_ref[rows, :].astype(BF16)
            gate = jnp.dot(x, wg, preferred_element_type=F32)
            up = jnp.dot(x, wu, preferred_element_type=F32)
            part = jnp.dot((_silu(gate) * up).astype(BF16), wd, preferred_element_type=F32)
            total = jnp.where(f == 0, 0.0, acc[rows, :]) + part
            acc[rows, :] = total
            y_ref[rows, :] = total

        sub_tile(0)
        for r in range(1, n_sub):
            pl.when(valid > r * MOE_SUB)(functools.partial(sub_tile, r))

    def zero_rows(r):
        y_ref[r * MOE_SUB:(r + 1) * MOE_SUB, :] = jnp.zeros((MOE_SUB, D_MODEL), y_ref.dtype)

    for r in range(n_sub):
        pl.when(last & (valid <= r * MOE_SUB))(functools.partial(zero_rows, r))


def moe_grouped(xs, te, tv, n_used, wg, wu, wd, tf):
    n_rows = xs.shape[0]
    E, _, F = wg.shape
    nf = F // tf

    def xrow(i, f, te, tv, nu):
        return (jnp.minimum(i, nu[0] - 1), 0)

    def fcol(i, f, nu):
        return jnp.where(i < nu[0], f, nf - 1)

    return pl.pallas_call(
        _moe_kernel,
        grid_spec=pltpu.PrefetchScalarGridSpec(
            num_scalar_prefetch=3,
            grid=(n_rows // MOE_TM, nf),
            in_specs=[pl.BlockSpec((MOE_TM, D_MODEL), xrow),
                      pl.BlockSpec((1, D_MODEL, tf),
                                   lambda i, f, te, tv, nu: (te[i], 0, fcol(i, f, nu))),
                      pl.BlockSpec((1, D_MODEL, tf),
                                   lambda i, f, te, tv, nu: (te[i], 0, fcol(i, f, nu))),
                      pl.BlockSpec((1, tf, D_MODEL),
                                   lambda i, f, te, tv, nu: (te[i], fcol(i, f, nu), 0))],
            out_specs=pl.BlockSpec((MOE_TM, D_MODEL), lambda i, f, te, tv, nu: (i, 0)),
            scratch_shapes=[pltpu.VMEM((MOE_TM, D_MODEL), F32)]),
        out_shape=jax.ShapeDtypeStruct((n_rows, D_MODEL), F32),
        compiler_params=_params("arbitrary", "arbitrary"),
        name="moe_grouped",
    )(te, tv, n_used, xs, wg, wu, wd)


def _combine_kernel(n_first, y0_ref, y1_ref, rt_ref, x_ref, mod_ref, gn_ref, xa_ref, xb_ref):
    f = rt_ref[:, 2:3] * y0_ref[...] + rt_ref[:, 3:4] * y1_ref[...]
    x2 = x_ref[...] + mod_ref[0, 5:6, :] * _rms(f, gn_ref[3:4, :])
    first = pl.program_id(0) < n_first

    @pl.when(first)
    def _():
        xa_ref[...] = x2

    @pl.when(jnp.logical_not(first))
    def _():
        xb_ref[...] = x2


def moe_combine(yg, route, x1, modt, gn, n_first):
    n_tok = x1.shape[0]
    n = n_tok // ROW_TILE
    row = lambda i: (i, 0)
    return pl.pallas_call(
        functools.partial(_combine_kernel, n_first),
        grid=(n,),
        in_specs=[pl.BlockSpec((ROW_TILE, D_MODEL), row),
                  pl.BlockSpec((ROW_TILE, D_MODEL), lambda i: (n + i, 0)),
                  pl.BlockSpec((ROW_TILE, LANES), row),
                  pl.BlockSpec((ROW_TILE, D_MODEL), row),
                  pl.BlockSpec((1, 6, D_MODEL), lambda i: (i, 0, 0)),
                  pl.BlockSpec((4, D_MODEL), lambda i: (0, 0))],
        out_specs=_two_part_specs(n_first, 0, 0),
        out_shape=(jax.ShapeDtypeStruct((n_first * ROW_TILE, D_MODEL), F32),
                   jax.ShapeDtypeStruct(((n - n_first) * ROW_TILE, D_MODEL), F32)),
        compiler_params=_params("arbitrary"),
        name="moe_combine",
    )(yg, yg, route, x1, modt, gn)


def _pack_w_in(w):
    d = w.shape[0]
    qw = N_Q_HEADS * HEAD_DIM
    kw = N_KV_HEADS * HEAD_DIM
    dw = DN_HEADS * DN_D
    o_k = qw
    o_v = o_k + kw
    o_qkv = o_v + kw
    o_z = o_qkv + 3 * dw
    o_ba = o_z + dw
    o_ga = o_ba + 4 * DN_HEADS
    o_gb = o_ga + D_MODEL
    main = jnp.concatenate([w[:, :qw], w[:, o_ga:o_gb], w[:, o_gb:], w[:, o_z:o_ba], w[:, o_qkv:o_z]],
                           axis=1)
    pad = jnp.zeros((d, N_SIDE - 2 * kw - (o_ga - o_ba)), w.dtype)
    side = jnp.concatenate([w[:, o_k:o_v], w[:, o_v:o_qkv], w[:, o_ba:o_ga], pad], axis=1)
    return main.astype(BF16), side.astype(BF16)


def _lane_consts(p):
    out = jnp.zeros((DN_HEADS, 1, LANES), F32)
    out = out.at[:, 0, 2].set(p[0])
    return out.at[:, 0, 3].set(p[1])


def kernel(x_prompt, x_sample, c, cache_k_l0, cache_v_l0, state_delta_l0, cache_k_l1, cache_v_l1,
           state_delta_l1, c_ctx, w_ada, b_ada, norm_g, w_in, attn_sink, conv_w, dn_a_log, dn_dt_bias,
           dn_norm_g, w_pa, w_pb, w_o, w_ff_gate, w_ff_up, w_ff_down, w_router, w_e_gate, w_e_up,
           w_e_down):
    Bc, Tc, _ = x_prompt.shape
    Bl, Tl, _ = x_sample.shape
    n_ctx = Bc * Tc
    n_tok = n_ctx + Bl * Tl
    depth = w_in.shape[0]
    caches = ((cache_k_l0, cache_v_l0, state_delta_l0), (cache_k_l1, cache_v_l1, state_delta_l1))

    x_parts = (x_prompt.reshape(n_ctx, D_MODEL), x_sample.reshape(Bl * Tl, D_MODEL), 0)
    cond =jnp.zeros((SUBLANES, D_MODEL), F32).at[:Bl].set(c).at[Bl].set(c_ctx)
    mod = ada_mod(cond, w_ada, b_ada)
    tile_row = np.concatenate([np.full(n_ctx // ROW_TILE, Bl),
                               np.repeat(np.arange(Bl), Tl // ROW_TILE)]).astype(np.int32)
    modt = mod[:, tile_row].reshape(depth, n_tok // ROW_TILE, 6, D_MODEL)
    cos, sin = rope_tables(Tl)

    outs = []
    h = norm_mod(x_parts[0], x_parts[1], norm_g[0, 0:1], modt[0])
    for l in range(depth):
        w_main, w_side = _pack_w_in(w_in[l])
        proj = matmul(h, w_main, BF16)
        side, k_out, v_out = side_proj(h, w_side, Bc, Tc)
        k_c, v_c, s_c = caches[l]
        oa_c = ctx_attention(proj, side, attn_sink[l], Bc, Tc)
        oa_l = lat_attention(proj, side, attn_sink[l], k_c, v_c, cos, sin, n_ctx, Bl, Tl)
        alog = _lane_consts(dn_a_log[l])
        dtb = _lane_consts(dn_dt_bias[l])
        gdn = dn_norm_g[l].reshape(1, DN_D)
        hps_c = min(DN_HEADS, DN_CHAINS // (2 * (Tc // DN_CHUNK)))
        hps_l = min(DN_HEADS, DN_CHAINS // (2 * (Tl // DN_CHUNK)))
        ob_c, s_fin = deltanet(proj, side, conv_w[l], alog, dtb, gdn, None, 0, Bc, Tc, hps_c)
        ob_l = deltanet(proj, side, conv_w[l], alog, dtb, gdn, s_c, n_ctx, Bl, Tl, hps_l)
        moe = l % 2 == 1
        i = l // 2
        wr = None
        if moe:
            wr = jnp.zeros((D_MODEL, LANES), F32).at[:, :N_EXPERTS].set(w_router[i])
        res = mix(oa_c, oa_l, ob_c, ob_l, proj, *x_parts, modt[l], norm_g[l], w_pa[l].astype(BF16),
                  w_pb[l].astype(BF16), w_o[l].astype(BF16), wr)
        if moe:
            x1, h2, route = res
            pos, n_rows, te, tv, n_used = route_plan(route, MOE_TM)
            xs = dispatch_rows(h2, pos, n_rows)
            ys = moe_grouped(xs, te, tv, n_used, w_e_gate[i], w_e_up[i], w_e_down[i], MOE_TF)
            yg = gather_rows(ys, pos)
            x_c, x_l = moe_combine(yg, route, x1, modt[l], norm_g[l], n_ctx // ROW_TILE)
            x_parts = (x_c, x_l, 0)
            h = None if l + 1 == depth else norm_mod(x_c, x_l, norm_g[l + 1, 0:1], modt[l + 1])
        else:
            x1, h2 = res
            nxt = l + 1 < depth
            r = ffn_dense(h2, w_ff_gate[i].astype(BF16), w_ff_up[i].astype(BF16),
                          w_ff_down[i].astype(BF16), x1, modt[l], norm_g[l],
                          modt[l + 1] if nxt else None, norm_g[l + 1] if nxt else None, FFN_TF)
            h = r[1] if nxt else None
            x_parts = (r[0], r[0], n_ctx // ROW_TILE)
        outs.append((k_out, v_out, s_fin))

    x_c, x_l, l_tile0 = x_parts
    y_prompt = x_c[:n_ctx].reshape(Bc, Tc, D_MODEL)
    y_sample = x_l[l_tile0 * ROW_TILE:].reshape(Bl, Tl, D_MODEL)
    return (y_prompt, y_sample, outs[0][0], outs[0][1], outs[0][2], outs[1][0], outs[1][1], outs[1][2])
```

```python
import functools

import numpy as np
import jax
import jax.numpy as jnp
from jax import lax
from jax.experimental import pallas as pl
from jax.experimental.pallas import tpu as pltpu
from jax.experimental.pallas import tpu_sc as plsc

F32 = jnp.float32
BF16 = jnp.bfloat16

D_MODEL = 1024
HEAD_DIM = 128
N_Q_HEADS = 8
N_KV_HEADS = 2
GQA = N_Q_HEADS // N_KV_HEADS
WINDOW = 128
GRID_W = 64
ROPE_BASE = 10000.0
DN_HEADS = 8
DN_D = 128
DN_CONV = 5
DN_CHUNK = 256
DN_LEVELS = (16, 64)
DN_GROUP = 8
DN_CHAINS = 16
N_EXPERTS = 8
TOP_K = 2
EPS = 1e-6
NEG_INF = -1e30

LANES = 128
SUBLANES = 8

ROW_TILE = 256
PROJ_TILE = 1024
FFN_TM = 512
FFN_TF = 1408
MOE_TF = 512
MOE_TM = 1024
MOE_SUB = 512
SC_WINDOW = 64
SC_CORES = 2
SC_SUBCORES = 16

C_Q = 0
C_GA = 1024
C_GB = 2048
C_Z = 3072
C_QB = 4096
C_KB = 5120
C_VB = 6144
N_MAIN = 7168
C_K = 0
C_V = 256
C_BA = 512
N_SIDE = 1024

VMEM_LIMIT = 56 * 1024 * 1024


def _params(*sem):
    return pltpu.CompilerParams(dimension_semantics=sem, vmem_limit_bytes=VMEM_LIMIT)


def _rms(x, g):
    return x * lax.rsqrt(jnp.mean(x * x, axis=-1, keepdims=True) + EPS) * g


def _silu(x):
    return x * jax.nn.sigmoid(x)


def _bdot(a, b):
    return jnp.dot(a.astype(BF16), b.astype(BF16), preferred_element_type=F32)


def _bdot_nt(a, b):
    return lax.dot_general(a.astype(BF16), b.astype(BF16), (((1,), (1,)), ((), ())),
                           preferred_element_type=F32)


def _pieces(x):
    p1 = x.astype(BF16)
    r1 = x - p1.astype(F32)
    p2 = r1.astype(BF16)
    return p1, p2, (r1 - p2.astype(F32)).astype(BF16)


def _dot_split_rhs(a, b):
    dot = functools.partial(jnp.dot, preferred_element_type=F32)
    b1, b2, b3 = _pieces(b)
    return dot(a, b1) + dot(a, b2) + dot(a, b3)


def _dot_split_both(a, b):
    dot = functools.partial(jnp.dot, preferred_element_type=F32)
    ah = a.astype(BF16)
    al = (a - ah.astype(F32)).astype(BF16)
    bh = b.astype(BF16)
    bl = (b - bh.astype(F32)).astype(BF16)
    return dot(ah, bh) + (dot(ah, bl) + dot(al, bh))


def _dot_split_lhs(a, b):
    dot = functools.partial(jnp.dot, preferred_element_type=F32)
    a1, a2, a3 = _pieces(a)
    return dot(a1, b) + dot(a2, b) + dot(a3, b)


def _ada_kernel(c_ref, w_ref, b_ref, o_ref):
    o_ref[0] = _dot_split_both(_silu(c_ref[...]), w_ref[0]) + b_ref[0]


def ada_mod(cond, w_ada, b_ada):
    L = w_ada.shape[0]
    n = w_ada.shape[2] // D_MODEL
    return pl.pallas_call(
        _ada_kernel,
        grid=(L, n),
        in_specs=[pl.BlockSpec((SUBLANES, D_MODEL), lambda l, j: (0, 0)),
                  pl.BlockSpec((1, D_MODEL, D_MODEL), lambda l, j: (l, 0, j)),
                  pl.BlockSpec((1, 1, D_MODEL), lambda l, j: (l, 0, j))],
        out_specs=pl.BlockSpec((1, SUBLANES, D_MODEL), lambda l, j: (l, 0, j)),
        out_shape=jax.ShapeDtypeStruct((L, SUBLANES, 6 * D_MODEL), F32),
        compiler_params=_params("parallel", "parallel"),
        name="ada_mod",
    )(cond, w_ada, b_ada.reshape(L, 1, 6 * D_MODEL))


def _two_part_specs(n_first, off_first, off_second):
    first = pl.BlockSpec((ROW_TILE, D_MODEL), lambda i: (jnp.minimum(i, n_first - 1) + off_first, 0))
    second = pl.BlockSpec((ROW_TILE, D_MODEL), lambda i: (jnp.maximum(i - n_first, 0) + off_second, 0))
    return first, second


def _norm_mod_kernel(n_first, xa_ref, xb_ref, g_ref, mod_ref, h_ref):
    x = jnp.where(pl.program_id(0) < n_first, xa_ref[...], xb_ref[...])
    h = _rms(x, g_ref[...]) * (1.0 + mod_ref[0, 1:2, :]) + mod_ref[0, 0:1, :]
    h_ref[...] = h.astype(h_ref.dtype)


def norm_mod(xa, xb, g, modt):
    na = xa.shape[0] // ROW_TILE
    n = na + xb.shape[0] // ROW_TILE
    return pl.pallas_call(
        functools.partial(_norm_mod_kernel, na),
        grid=(n,),
        in_specs=[*_two_part_specs(na, 0, 0),
                  pl.BlockSpec((1, D_MODEL), lambda i: (0, 0)),
                  pl.BlockSpec((1, 6, D_MODEL), lambda i: (i, 0, 0))],
        out_specs=pl.BlockSpec((ROW_TILE, D_MODEL), lambda i: (i, 0)),
        out_shape=jax.ShapeDtypeStruct((n * ROW_TILE, D_MODEL), BF16),
        compiler_params=_params("parallel"),
        name="norm_mod",
    )(xa, xb, g, modt)


def _mm_kernel(a_ref, b_ref, o_ref):
    o_ref[...] = jnp.dot(a_ref[...], b_ref[...], preferred_element_type=F32).astype(o_ref.dtype)


def matmul(a, b, out_dtype):
    M, K = a.shape
    N = b.shape[1]
    tm = tn = PROJ_TILE
    return pl.pallas_call(
        _mm_kernel,
        grid=(N // tn, M // tm),
        in_specs=[pl.BlockSpec((tm, K), lambda j, i: (i, 0)),
                  pl.BlockSpec((K, tn), lambda j, i: (0, j))],
        out_specs=pl.BlockSpec((tm, tn), lambda j, i: (i, j)),
        out_shape=jax.ShapeDtypeStruct((M, N), out_dtype),
        compiler_params=_params("parallel", "parallel"),
        name="in_proj",
    )(a, b)


def _side_kernel(n_ctx_tiles, seqs, T, a_ref, b_ref, o_ref, k_ref, v_ref):
    r = jnp.dot(a_ref[...], b_ref[...], preferred_element_type=F32)
    o_ref[...] = r

    @pl.when(pl.program_id(0) < n_ctx_tiles)
    def _():
        kw = N_KV_HEADS * HEAD_DIM
        k_ref[...] = r[:, C_K:C_K + kw].reshape(seqs, T, N_KV_HEADS, HEAD_DIM)
        v_ref[...] = r[:, C_V:C_V + kw].reshape(seqs, T, N_KV_HEADS, HEAD_DIM)


def side_proj(a, b, n_seq, T):
    M, K = a.shape
    tm = PROJ_TILE
    seqs = tm // T
    n_ctx_tiles = n_seq // seqs
    assert seqs * T == tm and n_ctx_tiles * seqs == n_seq
    kv_spec = pl.BlockSpec((seqs, T, N_KV_HEADS, HEAD_DIM),
                           lambda i: (jnp.minimum(i, n_ctx_tiles - 1), 0, 0, 0))
    kv_shape = jax.ShapeDtypeStruct((n_seq, T, N_KV_HEADS, HEAD_DIM), F32)
    return pl.pallas_call(
        functools.partial(_side_kernel, n_ctx_tiles, seqs, T),
        grid=(M // tm,),
        in_specs=[pl.BlockSpec((tm, K), lambda i: (i, 0)),
                  pl.BlockSpec((K, N_SIDE), lambda i: (0, 0))],
        out_specs=(pl.BlockSpec((tm, N_SIDE), lambda i: (i, 0)), kv_spec, kv_spec),
        out_shape=(jax.ShapeDtypeStruct((M, N_SIDE), F32), kv_shape, kv_shape),
        compiler_params=_params("arbitrary"),
        name="side_proj",
    )(a, b)


def _ctx_attn_kernel(sink_ref, q_ref, k_ref, v_ref, o_ref):
    hk = pl.program_id(1)
    k = k_ref[...].astype(BF16)
    v = v_ref[...].astype(BF16)
    scale = HEAD_DIM ** -0.5
    for g in range(GQA):
        q = q_ref[:, g * HEAD_DIM:(g + 1) * HEAD_DIM]
        s = _bdot_nt(q, k) * scale
        sink = sink_ref[hk * GQA + g]
        m = jnp.maximum(jnp.max(s, axis=-1, keepdims=True), sink)
        p = jnp.exp(s - m)
        den = jnp.sum(p, axis=-1, keepdims=True) + jnp.exp(sink - m)
        o = _bdot(p, v) / den
        o_ref[:, g * HEAD_DIM:(g + 1) * HEAD_DIM] = o.astype(o_ref.dtype)


def ctx_attention(proj, side, sink, n_seq, T):
    qw = GQA * HEAD_DIM
    return pl.pallas_call(
        _ctx_attn_kernel,
        grid_spec=pltpu.PrefetchScalarGridSpec(
            num_scalar_prefetch=1,
            grid=(n_seq, N_KV_HEADS),
            in_specs=[pl.BlockSpec((T, qw), lambda b, h, s: (b, C_Q // qw + h)),
                      pl.BlockSpec((T, HEAD_DIM), lambda b, h, s: (b, C_K // HEAD_DIM + h)),
                      pl.BlockSpec((T, HEAD_DIM), lambda b, h, s: (b, C_V // HEAD_DIM + h))],
            out_specs=pl.BlockSpec((T, qw), lambda b, h, s: (b, h))),
        out_shape=jax.ShapeDtypeStruct((n_seq * T, N_Q_HEADS * HEAD_DIM), BF16),
        compiler_params=_params("parallel", "parallel"),
        name="ctx_attn",
    )(sink, proj, side, side)


def _rope(x, cos, sin):
    lane = lax.broadcasted_iota(jnp.int32, x.shape, 1)
    q = HEAD_DIM // 4
    swapped = jnp.where((lane % (2 * q)) < q,
                        pltpu.roll(x, HEAD_DIM - q, 1), pltpu.roll(x, q, 1))
    return x * cos + swapped * sin


def _lat_attn_kernel(T, sink_ref, q_ref, k_ref, v_ref, kc_ref, vc_ref, cosq_ref, sinq_ref,
                     cos_ref, sin_ref, o_ref, kpad, vpad):
    hk = pl.program_id(1)
    i = pl.program_id(2)
    B = WINDOW

    @pl.when(i == 0)
    def _():
        zeros = jnp.zeros((B, HEAD_DIM), BF16)
        kpad[0:B, :] = zeros
        kpad[T + B:T + 2 * B, :] = zeros
        vpad[0:B, :] = zeros
        vpad[T + B:T + 2 * B, :] = zeros
        kpad[B:T + B, :] = _rope(k_ref[...], cos_ref[...], sin_ref[...]).astype(BF16)
        vpad[B:T + B, :] = v_ref[...].astype(BF16)

    start = pl.multiple_of(i * B, B)
    kw = kpad[pl.ds(start, 3 * B), :]
    vw = vpad[pl.ds(start, 3 * B), :]
    kc = kc_ref[0].astype(BF16)
    vc = vc_ref[0].astype(BF16)

    rows = GQA * B
    qi = lax.broadcasted_iota(jnp.int32, (rows, B), 0) % B
    kj = lax.broadcasted_iota(jnp.int32, (rows, B), 1)
    mask_prev = (kj >= qi) & (i > 0)
    mask_next = (kj <= qi) & (i < pl.num_programs(2) - 1)

    cq = cosq_ref[...]
    sq = sinq_ref[...]
    scale = HEAD_DIM ** -0.5
    q = jnp.concatenate(
        [_rope(q_ref[:, g * HEAD_DIM:(g + 1) * HEAD_DIM].astype(F32), cq, sq) for g in range(GQA)],
        axis=0) * scale
    s_c = _bdot_nt(q, kc)
    s_w = _bdot_nt(q, kw)
    s_w = jnp.concatenate([jnp.where(mask_prev, s_w[:, 0:B], NEG_INF), s_w[:, B:2 * B],
                           jnp.where(mask_next, s_w[:, 2 * B:3 * B], NEG_INF)], axis=1)
    gi = lax.broadcasted_iota(jnp.int32, (rows, 1), 0) // B
    sink = jnp.zeros((rows, 1), F32)
    for g in range(GQA):
        sink = jnp.where(gi == g, sink_ref[hk * GQA + g], sink)
    m = jnp.maximum(jnp.maximum(jnp.max(s_c, axis=-1, keepdims=True),
                                jnp.max(s_w, axis=-1, keepdims=True)), sink)
    p_c = jnp.exp(s_c - m)
    p_w = jnp.exp(s_w - m)
    den = (jnp.sum(p_c, axis=-1, keepdims=True) + jnp.sum(p_w, axis=-1, keepdims=True)
           + jnp.exp(sink - m))
    o = (_bdot(p_c, vc) + _bdot(p_w, vw)) / den
    for g in range(GQA):
        o_ref[:, g * HEAD_DIM:(g + 1) * HEAD_DIM] = o[g * B:(g + 1) * B, :].astype(o_ref.dtype)


def lat_attention(proj, side, sink, cache_k, cache_v, cos, sin, row0, n_seq, T):
    qw = GQA * HEAD_DIM
    B = WINDOW
    nb = T // B
    P = cache_k.shape[1]
    kc = cache_k.reshape(n_seq, P, N_KV_HEADS * HEAD_DIM)
    vc = cache_v.reshape(n_seq, P, N_KV_HEADS * HEAD_DIM)
    rb = row0 // B
    rt = row0 // T
    return pl.pallas_call(
        functools.partial(_lat_attn_kernel, T),
        grid_spec=pltpu.PrefetchScalarGridSpec(
            num_scalar_prefetch=1,
            grid=(n_seq, N_KV_HEADS, nb),
            in_specs=[pl.BlockSpec((B, qw), lambda b, h, i, s: (rb + b * nb + i, C_Q // qw + h)),
                      pl.BlockSpec((T, HEAD_DIM), lambda b, h, i, s: (rt + b, C_K // HEAD_DIM + h)),
                      pl.BlockSpec((T, HEAD_DIM), lambda b, h, i, s: (rt + b, C_V // HEAD_DIM + h)),
                      pl.BlockSpec((1, P, HEAD_DIM), lambda b, h, i, s: (b, 0, h)),
                      pl.BlockSpec((1, P, HEAD_DIM), lambda b, h, i, s: (b, 0, h)),
                      pl.BlockSpec((B, HEAD_DIM), lambda b, h, i, s: (i, 0)),
                      pl.BlockSpec((B, HEAD_DIM), lambda b, h, i, s: (i, 0)),
                      pl.BlockSpec((T, HEAD_DIM), lambda b, h, i, s: (0, 0)),
                      pl.BlockSpec((T, HEAD_DIM), lambda b, h, i, s: (0, 0))],
            out_specs=pl.BlockSpec((B, qw), lambda b, h, i, s: (b * nb + i, h)),
            scratch_shapes=[pltpu.VMEM((T + 2 * B, HEAD_DIM), BF16),
                            pltpu.VMEM((T + 2 * B, HEAD_DIM), BF16)]),
        out_shape=jax.ShapeDtypeStruct((n_seq * T, N_Q_HEADS * HEAD_DIM), BF16),
        compiler_params=_params("parallel", "parallel", "arbitrary"),
        name="lat_attn",
    )(sink, proj, side, side, kc, vc, cos, sin, cos, sin)


def rope_tables(T):
    rows = (np.arange(T) // GRID_W).astype(np.float32)
    cols = (np.arange(T) % GRID_W).astype(np.float32)
    n_freq = HEAD_DIM // 4
    inv = jnp.asarray(ROPE_BASE, F32) ** (-jnp.arange(n_freq, dtype=F32) / n_freq)
    ar = jnp.asarray(rows)[:, None] * inv
    ac = jnp.asarray(cols)[:, None] * inv
    cos = jnp.concatenate([jnp.cos(ar), jnp.cos(ar), jnp.cos(ac), jnp.cos(ac)], axis=-1)
    sin = jnp.concatenate([-jnp.sin(ar), jnp.sin(ar), -jnp.sin(ac), jnp.sin(ac)], axis=-1)
    return cos, sin


def _dn_kernel(T, hps, has_s0, *refs):
    n_in = 12 if has_s0 else 11
    (q_ref, k_ref, v_ref, z_ref, ba_ref, wq_ref, wk_ref, wv_ref, alog_ref, dt_ref, gn_ref) = refs[:11]
    s0_ref = refs[11] if has_s0 else None
    if has_s0:
        o_ref = refs[n_in]
        sfin_ref = None
        scratch = refs[n_in + 1:]
    else:
        o_ref, sfin_ref = refs[n_in], refs[n_in + 1]
        scratch = refs[n_in + 2:]
    xpad, qs, ks, vs, cols, gcs, gcT, kTs, U, WQ, QKM, KDT, EGL, OST = scratch

    hg = pl.program_id(1)
    C = DN_CHUNK
    D = DN_D
    n = T // C
    half = DN_CONV // 2
    PAD = SUBLANES

    zpad = jnp.zeros((PAD, D), F32)
    xpad[0:PAD, :] = zpad
    xpad[T + PAD:T + 2 * PAD, :] = zpad

    def conv_silu(x_ref, w_ref, hh):
        xpad[PAD:T + PAD, :] = x_ref[:, hh * D:(hh + 1) * D].astype(F32)
        w = w_ref[:, hh * D:(hh + 1) * D]
        acc = xpad[pl.ds(PAD - half, T), :] * w[0:1, :]
        for j in range(1, DN_CONV):
            acc = acc + xpad[pl.ds(PAD - half + j, T), :] * w[j:j + 1, :]
        return _silu(acc)

    def l2n(x):
        return x * lax.rsqrt(jnp.sum(x * x, axis=-1, keepdims=True) + EPS)

    ri = lax.broadcasted_iota(jnp.int32, (C, C), 0)
    ci = lax.broadcasted_iota(jnp.int32, (C, C), 1)
    eye = jnp.where(ri == ci, 1.0, 0.0).astype(F32)
    level_mask = []
    inside_prev = None
    for size in DN_LEVELS + (C,):
        inside = (ri // size) == (ci // size)
        joins = inside if inside_prev is None else inside & jnp.logical_not(inside_prev)
        level_mask.append(jnp.where(joins, 1.0, 0.0).astype(BF16))
        inside_prev = inside
    lower = jnp.where(ri >= ci, 1.0, 0.0).astype(BF16)
    li = lax.broadcasted_iota(jnp.int32, (LANES, LANES), 0)
    lj = lax.broadcasted_iota(jnp.int32, (LANES, LANES), 1)
    lane_t = lax.broadcasted_iota(jnp.int32, (T, LANES), 1)
    lane_c = lax.broadcasted_iota(jnp.int32, (C, LANES), 1)

    def prologue(hh):
        def conv_q():
            qs[hh] = l2n(conv_silu(q_ref, wq_ref, hh)) * (D ** -0.5)

        def conv_k():
            ks[hh] = l2n(conv_silu(k_ref, wk_ref, hh))

        def conv_v():
            vs[hh] = conv_silu(v_ref, wv_ref, hh)

        def gates():
            head = hg * hps + hh
            sel = jnp.where((lj < 4) & (li == lj * DN_HEADS + head), 1.0, 0.0).astype(BF16)
            raw = _dot_split_lhs(ba_ref[...], sel)
            gate = -jnp.exp(alog_ref[hh]) * jax.nn.softplus(raw + dt_ref[hh])
            cols[hh] = jnp.where(lane_t < 2, jax.nn.sigmoid(raw), gate)

        def cumulative(c):
            rows = slice(c * C, (c + 1) * C)
            g = cols[hh, rows, :]
            pre = _dot_split_rhs(lower, g)
            gc = jnp.where(lane_c == 3, pre[C - 1:C, :] - pre + g, pre)
            gcs[hh, rows, :] = gc
            gcT[hh * n + c] = gc.T[0:SUBLANES, :]
            kTs[hh * n + c] = ks[hh, rows, :].T

        return [conv_q, conv_k, conv_v, gates] + [functools.partial(cumulative, c) for c in range(n)]

    chains = [(hh, c, d) for hh in range(hps) for c in range(n) for d in (0, 1)]
    chain_id = {ch: j for j, ch in enumerate(chains)}

    def nilpotent_inverse(ms, squarings):
        dot = functools.partial(jnp.dot, preferred_element_type=F32)
        ts = [eye + m.astype(F32) for m in ms]
        pws = ms
        for _ in range(squarings):
            pws = [dot(pw, pw).astype(BF16) for pw in pws]
            ts = [t + dot(t.astype(BF16), pw) for t, pw in zip(ts, pws)]
        return ts

    def phase_a(j0, grp):
        G = range(len(grp))
        rows = [slice(c * C, (c + 1) * C) for (_, c, _) in grp]
        incl = [(ri >= ci) if d == 0 else (ri <= ci) for (_, _, d) in grp]
        strict = [(ri > ci) if d == 0 else (ri < ci) for (_, _, d) in grp]
        q = [qs[grp[i][0], rows[i], :] for i in G]
        k = [ks[grp[i][0], rows[i], :] for i in G]
        v = [vs[grp[i][0], rows[i], :] for i in G]
        gcc = [gcs[hh, rows[i], 2 + d:3 + d] for i, (hh, _, d) in enumerate(grp)]
        bcol = [cols[hh, rows[i], d:d + 1] for i, (hh, _, d) in enumerate(grp)]
        kT = [kTs[hh * n + c] for (hh, c, _) in grp]
        gcr = [gcT[hh * n + c, 2 + d:3 + d, :] for (hh, c, d) in grp]
        g_last = [gcr[i][:, C - 1:C] if grp[i][2] == 0 else gcr[i][:, 0:1] for i in G]
        decay = [jnp.where(incl[i], jnp.exp(jnp.where(incl[i], gcc[i] - gcr[i], 0.0)), 0.0) for i in G]
        qkk_of = {}
        for i in G:
            if grp[i][:2] not in qkk_of:
                qkk_of[grp[i][:2]] = _bdot(jnp.concatenate([q[i], k[i]], axis=0), kT[i])
        qkk = [qkk_of[grp[i][:2]] for i in G]
        a_neg = [jnp.where(strict[i], -(qkk[i][C:2 * C, :] * bcol[i] * decay[i]), 0.0).astype(BF16)
                 for i in G]
        sizes = DN_LEVELS + (C,)
        t_inv = nilpotent_inverse([a_neg[i] * level_mask[0] for i in G], sizes[0].bit_length() - 2)
        for lvl in range(1, len(sizes)):
            e_neg = [_bdot(t_inv[i], a_neg[i] * level_mask[lvl]).astype(BF16) for i in G]
            f = nilpotent_inverse(e_neg, (sizes[lvl] // sizes[lvl - 1]).bit_length() - 2)
            t_inv = [_bdot(f[i], t_inv[i]) for i in G]
        from_zero = (not has_s0) and all(c == (0 if d == 0 else n - 1) for (_, c, d) in grp)
        r0, r1 = j0, j0 + len(grp)
        if from_zero:
            uw = [_bdot(t_inv[i], v[i] * bcol[i]) for i in G]
        else:
            egc = [jnp.exp(gcc[i]) for i in G]
            uw = [_bdot(t_inv[i], jnp.concatenate([v[i] * bcol[i], k[i] * (bcol[i] * egc[i])], axis=1))
                  for i in G]
            WQ[r0 * 2 * C:r1 * 2 * C, :] = jnp.concatenate(
                [jnp.concatenate([uw[i][:, D:2 * D], q[i] * egc[i]], axis=0) for i in G],
                axis=0).astype(BF16)
        U[r0 * C:r1 * C, :] = jnp.concatenate([uw[i][:, 0:D] for i in G], axis=0)
        QKM[r0 * C:r1 * C, :] = jnp.concatenate(
            [qkk[i][0:C, :] * decay[i] for i in G], axis=0).astype(BF16)
        KDT[r0 * D:r1 * D, :] = jnp.concatenate(
            [kT[i] * jnp.exp(g_last[i] - gcr[i]) for i in G], axis=0).astype(BF16)
        EGL[r0 * SUBLANES:r1 * SUBLANES, :] = jnp.concatenate(
            [jnp.broadcast_to(jnp.exp(g_last[i]), (SUBLANES, LANES)) for i in G], axis=0)

    for hh in range(hps):
        for step in prologue(hh):
            step()
    for j0 in range(0, len(chains), DN_GROUP):
        phase_a(j0, chains[j0:j0 + DN_GROUP])

    if has_s0:
        state = [s0_ref[0, d, hh] for hh in range(hps) for d in (0, 1)]
    else:
        state = [None] * (2 * hps)
    for i in range(n):
        outs = []
        new_state = []
        for hh in range(hps):
            for d in (0, 1):
                j = chain_id[(hh, i if d == 0 else n - 1 - i, d)]
                s_old = state[hh * 2 + d]
                if s_old is None:
                    v_new = U[j * C:(j + 1) * C, :].astype(BF16)
                    outs.append(jnp.dot(QKM[j * C:(j + 1) * C, :], v_new, preferred_element_type=F32))
                    new_state.append(jnp.dot(KDT[j * D:(j + 1) * D, :], v_new,
                                             preferred_element_type=F32))
                    continue
                xs =jnp.dot(WQ[j * 2 * C:(j + 1) * 2 * C, :], s_old.astype(BF16),
                             preferred_element_type=F32)
                v_new = (U[j * C:(j + 1) * C, :] - xs[0:C, :]).astype(BF16)
                outs.append(xs[C:2 * C, :] + jnp.dot(QKM[j * C:(j + 1) * C, :], v_new,
                                                     preferred_element_type=F32))
                new_state.append(s_old * EGL[j * SUBLANES:j * SUBLANES + 1, 0:1]
                                 + jnp.dot(KDT[j * D:(j + 1) * D, :], v_new, preferred_element_type=F32))
        OST[i] = jnp.concatenate(outs, axis=0)
        state = new_state

    if not has_s0:
        for hh in range(hps):
            for d in (0, 1):
                sfin_ref[0, d, hh] = state[hh * 2 + d]

    for hh in range(hps):
        for c in range(n):
            rows = slice(c * C, (c + 1) * C)
            o = (OST[c, (2 * hh) * C:(2 * hh + 1) * C, :]
                 + OST[n - 1 - c, (2 * hh + 1) * C:(2 * hh + 2) * C, :])
            y = _rms(o, gn_ref[...]) * _silu(z_ref[rows, hh * D:(hh + 1) * D].astype(F32))
            o_ref[rows, hh * D:(hh + 1) * D] = y.astype(o_ref.dtype)


def deltanet(proj, side, conv_w, alog_l, dt_l, gn, s0, row0, n_seq, T, hps):
    rt = row0 // T
    has_s0 = s0 is not None
    W = hps * DN_D
    n = T // DN_CHUNK
    n_chain = hps * n * 2

    def col(c0):
        return pl.BlockSpec((T, W), lambda b, h: (rt + b, c0 // W + h))

    def wcol(c0):
        return pl.BlockSpec((DN_CONV, W), lambda b, h: (0, c0 // W + h))

    in_specs = [col(C_QB), col(C_KB), col(C_VB), col(C_Z),
                pl.BlockSpec((T, LANES), lambda b, h: (rt + b, C_BA // LANES)),
                wcol(0), wcol(DN_HEADS * DN_D), wcol(2 * DN_HEADS * DN_D),
                pl.BlockSpec((hps, 1, LANES), lambda b, h: (h, 0, 0)),
                pl.BlockSpec((hps, 1, LANES), lambda b, h: (h, 0, 0)),
                pl.BlockSpec((1, DN_D), lambda b, h: (0, 0))]
    args = [proj, proj, proj, proj, side, conv_w, conv_w, conv_w, alog_l, dt_l, gn]
    o_spec = pl.BlockSpec((T, W), lambda b, h: (b, h))
    o_shape = jax.ShapeDtypeStruct((n_seq * T, DN_HEADS * DN_D), BF16)
    s_spec = pl.BlockSpec((1, 2, hps, DN_D, DN_D), lambda b, h: (b, 0, h, 0, 0))
    if has_s0:
        in_specs.append(s_spec)
        args.append(s0)
        out_specs, out_shape = o_spec, o_shape
    else:
        out_specs = (o_spec, s_spec)
        out_shape = (o_shape, jax.ShapeDtypeStruct((n_seq, 2, DN_HEADS, DN_D, DN_D), F32))
    C = DN_CHUNK
    scratch = [pltpu.VMEM((T + 2 * SUBLANES, DN_D), F32),
               pltpu.VMEM((hps, T, DN_D), F32),
               pltpu.VMEM((hps, T, DN_D), F32),
               pltpu.VMEM((hps, T, DN_D), F32),
               pltpu.VMEM((hps, T, LANES), F32),
               pltpu.VMEM((hps, T, LANES), F32),
               pltpu.VMEM((hps * n, SUBLANES, C), F32),
               pltpu.VMEM((hps * n, DN_D, C), F32),
               pltpu.VMEM((n_chain * C, DN_D), F32),
               pltpu.VMEM((n_chain * 2 * C, DN_D), BF16),
               pltpu.VMEM((n_chain * C, C), BF16),
               pltpu.VMEM((n_chain * DN_D, C), BF16),
               pltpu.VMEM((n_chain * SUBLANES, LANES), F32),
               pltpu.VMEM((n, 2 * hps * C, DN_D), F32)]
    return pl.pallas_call(
        functools.partial(_dn_kernel, T, hps, has_s0),
        grid=(n_seq, DN_HEADS // hps),
        in_specs=in_specs,
        out_specs=out_specs,
        out_shape=out_shape,
        scratch_shapes=scratch,
        compiler_params=_params("parallel", "parallel"),
        name="deltanet_lat" if has_s0 else "deltanet_ctx",
    )(*args)


def _route_top2(logits):
    lane = lax.broadcasted_iota(jnp.int32, logits.shape, 1).astype(F32)
    lg = jnp.where(lane < N_EXPERTS, logits, -jnp.inf)
    m1 = jnp.max(lg, axis=-1, keepdims=True)
    i1 = jnp.min(jnp.where(lg == m1, lane, float(LANES)), axis=-1, keepdims=True)
    lg2 = jnp.where(lane == i1, -jnp.inf, lg)
    m2 = jnp.max(lg2, axis=-1, keepdims=True)
    i2 = jnp.min(jnp.where(lg2 == m2, lane, float(LANES)), axis=-1, keepdims=True)
    e2 = jnp.exp(m2 - m1)
    den = 1.0 + e2
    return jnp.where(lane == 0.0, i1, jnp.where(lane == 1.0, i2,
                     jnp.where(lane == 2.0, 1.0 / den, jnp.where(lane == 3.0, e2 / den, 0.0))))


def _mix_kernel(route, n_ctx_tiles, oac_ref, oal_ref, obc_ref, obl_ref, ga_ref, gb_ref, xc_ref,
                xl_ref, mod_ref, gn_ref, wpa_ref, wpb_ref, wo_ref, *rest):
    if route:
        wr_ref, x1_ref, h2_ref, rt_ref = rest
    else:
        x1_ref, h2_ref = rest
    is_ctx = pl.program_id(0) < n_ctx_tiles
    o_a = jnp.where(is_ctx, oac_ref[...], oal_ref[...])
    o_b = jnp.where(is_ctx, obc_ref[...], obl_ref[...])
    x = jnp.where(is_ctx, xc_ref[...], xl_ref[...])
    br_a = jnp.dot(o_a, wpa_ref[...], preferred_element_type=F32)
    br_b = jnp.dot(o_b, wpb_ref[...], preferred_element_type=F32)
    merged = (jax.nn.sigmoid(ga_ref[...].astype(F32)) * br_a
              + jax.nn.sigmoid(gb_ref[...].astype(F32)) * br_b)
    y = jnp.dot(merged.astype(BF16), wo_ref[...], preferred_element_type=F32)
    x1 = x + mod_ref[0, 2:3, :] * _rms(y, gn_ref[1:2, :])
    x1_ref[...] = x1
    h2 = _rms(x1, gn_ref[2:3, :]) * (1.0 + mod_ref[0, 4:5, :]) + mod_ref[0, 3:4, :]
    h2_ref[...] = h2.astype(h2_ref.dtype)
    if route:
        rt_ref[...] = _route_top2(_dot_split_both(h2, wr_ref[...]))


def mix(oa_c, oa_l, ob_c, ob_l, proj, x_c, x_l, x_l_tile0, modt, gn, wpa, wpb, wo, w_router):
    nc = oa_c.shape[0] // ROW_TILE
    nl = oa_l.shape[0] // ROW_TILE
    n = nc + nl
    n_tok = n * ROW_TILE
    route = w_router is not None
    row = lambda i: (i, 0)
    full = lambda i: (0, 0)
    crow, lrow = _two_part_specs(nc, 0, 0)
    in_specs = [crow, lrow, crow, lrow,
                pl.BlockSpec((ROW_TILE, D_MODEL), lambda i: (i, C_GA // D_MODEL)),
                pl.BlockSpec((ROW_TILE, D_MODEL), lambda i: (i, C_GB // D_MODEL)),
                *_two_part_specs(nc, 0, x_l_tile0),
                pl.BlockSpec((1, 6, D_MODEL), lambda i: (i, 0, 0)),
                pl.BlockSpec((4, D_MODEL), full),
                pl.BlockSpec((D_MODEL, D_MODEL), full),
                pl.BlockSpec((D_MODEL, D_MODEL), full),
                pl.BlockSpec((D_MODEL, D_MODEL), full)]
    args = [oa_c, oa_l, ob_c, ob_l, proj, proj, x_c, x_l, modt, gn, wpa, wpb, wo]
    out_specs = [pl.BlockSpec((ROW_TILE, D_MODEL), row), pl.BlockSpec((ROW_TILE, D_MODEL), row)]
    out_shape = [jax.ShapeDtypeStruct((n_tok, D_MODEL), F32),
                 jax.ShapeDtypeStruct((n_tok, D_MODEL), F32 if route else BF16)]
    if route:
        in_specs.append(pl.BlockSpec((D_MODEL, LANES), full))
        args.append(w_router)
        out_specs.append(pl.BlockSpec((ROW_TILE, LANES), row))
        out_shape.append(jax.ShapeDtypeStruct((n_tok, LANES), F32))
    return pl.pallas_call(
        functools.partial(_mix_kernel, route, nc),
        grid=(n,),
        in_specs=in_specs,
        out_specs=tuple(out_specs),
        out_shape=tuple(out_shape),
        compiler_params=_params("parallel"),
        name="mix_route" if route else "mix",
    )(*args)


def _ffn_kernel(next_norm, h_ref, wg_ref, wu_ref, wd_ref, x_ref, mod_ref, gn_ref, *rest):
    if next_norm:
        modn_ref, gnn_ref, x2_ref, hn_ref, acc = rest
    else:
        x2_ref, acc = rest
    f = pl.program_id(1)
    h = h_ref[...]
    gate = jnp.dot(h, wg_ref[...], preferred_element_type=F32)
    up = jnp.dot(h, wu_ref[...], preferred_element_type=F32)
    part = jnp.dot((_silu(gate) * up).astype(BF16), wd_ref[...], preferred_element_type=F32)
    acc[...] = jnp.where(f == 0, 0.0, acc[...]) + part

    @pl.when(f == pl.num_programs(1) - 1)
    def _():
        for r in range(FFN_TM // ROW_TILE):
            rs = slice(r * ROW_TILE, (r + 1) * ROW_TILE)
            x2 = x_ref[rs, :] + mod_ref[r, 5:6, :] * _rms(acc[rs, :], gn_ref[3:4, :])
            x2_ref[rs, :] = x2
            if next_norm:
                hn = _rms(x2, gnn_ref[0:1, :]) * (1.0 + modn_ref[r, 1:2, :]) + modn_ref[r, 0:1, :]
                hn_ref[rs, :] = hn.astype(hn_ref.dtype)


def ffn_dense(h2, wg, wu, wd, x1, modt, gn, modt_next, gn_next, tf):
    n_tok = x1.shape[0]
    F = wg.shape[1]
    nr = FFN_TM // ROW_TILE
    next_norm = modt_next is not None
    row = lambda i, f: (i, 0)
    in_specs = [pl.BlockSpec((FFN_TM, D_MODEL), row),
                pl.BlockSpec((D_MODEL, tf), lambda i, f: (0, f)),
                pl.BlockSpec((D_MODEL, tf), lambda i, f: (0, f)),
                pl.BlockSpec((tf, D_MODEL), lambda i, f: (f, 0)),
                pl.BlockSpec((FFN_TM, D_MODEL), row),
                pl.BlockSpec((nr, 6, D_MODEL), lambda i, f: (i, 0, 0)),
                pl.BlockSpec((4, D_MODEL), lambda i, f: (0, 0))]
    args = [h2, wg, wu, wd, x1, modt, gn]
    out_specs = [pl.BlockSpec((FFN_TM, D_MODEL), row)]
    out_shape = [jax.ShapeDtypeStruct((n_tok, D_MODEL), F32)]
    if next_norm:
        in_specs += [pl.BlockSpec((nr, 6, D_MODEL), lambda i, f: (i, 0, 0)),
                     pl.BlockSpec((4, D_MODEL), lambda i, f: (0, 0))]
        args += [modt_next, gn_next]
        out_specs.append(pl.BlockSpec((FFN_TM, D_MODEL), row))
        out_shape.append(jax.ShapeDtypeStruct((n_tok, D_MODEL), BF16))
    return pl.pallas_call(
        functools.partial(_ffn_kernel, next_norm),
        grid=(n_tok // FFN_TM, F // tf),
        in_specs=in_specs,
        out_specs=tuple(out_specs),
        out_shape=tuple(out_shape),
        scratch_shapes=[pltpu.VMEM((FFN_TM, D_MODEL), F32)],
        compiler_params=_params("parallel", "arbitrary"),
        name="ffn_dense",
    )(*args)


def gather_rows(x, idx):
    n = idx.shape[0]
    w = x.shape[1]
    per_worker = n // (SC_CORES * SC_SUBCORES)
    steps = per_worker // SC_WINDOW
    assert steps * SC_WINDOW * SC_CORES * SC_SUBCORES == n
    mesh = plsc.VectorSubcoreMesh(core_axis_name="core", subcore_axis_name="subcore")

    @functools.partial(pl.kernel, out_type=jax.ShapeDtypeStruct((n, w), x.dtype), mesh=mesh,
                       scratch_types=[pltpu.VMEM((SC_WINDOW,), jnp.int32),
                                      pltpu.VMEM((SC_WINDOW, w), x.dtype),
                                      pltpu.SemaphoreType.DMA])
    def gather_kernel(x_hbm, i_hbm, o_hbm, idx_v, rows_v, sem):
        worker = lax.axis_index("subcore") * SC_CORES + lax.axis_index("core")
        base = worker * per_worker

        @pl.loop(0, steps)
        def _(s):
            off = base + s * SC_WINDOW
            pltpu.sync_copy(i_hbm.at[pl.ds(off, SC_WINDOW)], idx_v)
            pltpu.async_copy(x_hbm.at[idx_v], rows_v, sem).wait()
            pltpu.sync_copy(rows_v, o_hbm.at[pl.ds(off, SC_WINDOW)])

    return gather_kernel(x, idx)


def dispatch_rows(x, pos, n_rows):
    n_tok, w = x.shape
    per_worker = n_tok // (SC_CORES * SC_SUBCORES)
    steps = per_worker // SC_WINDOW
    assert steps * SC_WINDOW * SC_CORES * SC_SUBCORES == n_tok
    mesh = plsc.VectorSubcoreMesh(core_axis_name="core", subcore_axis_name="subcore")

    @functools.partial(pl.kernel, out_type=jax.ShapeDtypeStruct((n_rows, w), x.dtype), mesh=mesh,
                       scratch_types=[pltpu.VMEM((SC_WINDOW,), jnp.int32),
                                      pltpu.VMEM((SC_WINDOW, w), x.dtype)])
    def dispatch_kernel(x_hbm, p_hbm, o_hbm, idx_v, rows_v):
        worker = lax.axis_index("subcore") * SC_CORES + lax.axis_index("core")
        base = worker * per_worker

        @pl.loop(0, steps)
        def _(s):
            off = base + s * SC_WINDOW
            pltpu.sync_copy(x_hbm.at[pl.ds(off, SC_WINDOW)], rows_v)
            for k in range(TOP_K):
                pltpu.sync_copy(p_hbm.at[pl.ds(k * n_tok + off, SC_WINDOW)], idx_v)
                pltpu.sync_copy(rows_v, o_hbm.at[idx_v])

    return dispatch_kernel(x, pos)


def route_plan(route, tm):
    n_tok = route.shape[0]
    ef = route[:, 0:TOP_K].astype(jnp.int32).T.reshape(-1)
    onehot = (ef[:, None] == jnp.arange(N_EXPERTS, dtype=jnp.int32)[None, :]).astype(jnp.int32)
    csum = jnp.cumsum(onehot, axis=0)
    rank = jnp.sum(csum * onehot, axis=1) - 1
    counts = csum[-1]
    padded = ((counts + tm - 1) // tm) * tm
    ends = jnp.cumsum(padded)
    pos = (ends - padded)[ef] + rank
    n_rows = TOP_K * n_tok + N_EXPERTS * tm
    n_tiles = n_rows // tm
    n_used = (ends[-1] // tm).astype(jnp.int32)
    tile_start = jnp.arange(n_tiles, dtype=jnp.int32) * tm
    te = jnp.sum((tile_start[:, None] >= ends[None, :]).astype(jnp.int32), axis=1)
    te = jnp.where(jnp.arange(n_tiles) < n_used, te, te[n_used - 1])
    real_end = ends - padded + counts
    tv = jnp.clip(real_end[te] - tile_start, 0, tm)
    tv = jnp.where(jnp.arange(n_tiles) < n_used, tv, 0)
    return pos.astype(jnp.int32), n_rows, te.astype(jnp.int32), tv.astype(jnp.int32), n_used.reshape(1)


def _moe_kernel(te_ref, tv_ref, nu_ref, x_ref, wg_ref, wu_ref, wd_ref, y_ref, acc):
    i = pl.program_id(0)
    f = pl.program_id(1)
    last = f == pl.num_programs(1) - 1
    used = i < nu_ref[0]
    valid = tv_ref[i]
    n_sub = MOE_TM // MOE_SUB

    @pl.when(used)
    def _():
        wg = wg_ref[0].astype(BF16)
        wu = wu_ref[0].astype(BF16)
        wd = wd_ref[0].astype(BF16)

        def sub_tile(r):
            rows = slice(r * MOE_SUB, (r + 1) * MOE_SUB)
            x = x_ref[rows, :].astype(BF16)
            gate = jnp.dot(x, wg, preferred_element_type=F32)
            up = jnp.dot(x, wu, preferred_element_type=F32)
            part = jnp.dot((_silu(gate) * up).astype(BF16), wd, preferred_element_type=F32)
            total = jnp.where(f == 0, 0.0, acc[rows, :]) + part
            acc[rows, :] = total
            y_ref[rows, :] = total

        sub_tile(0)
        for r in range(1, n_sub):
            pl.when(valid > r * MOE_SUB)(functools.partial(sub_tile, r))

    def zero_rows(r):
        y_ref[r * MOE_SUB:(r + 1) * MOE_SUB, :] = jnp.zeros((MOE_SUB, D_MODEL), y_ref.dtype)

    for r in range(n_sub):
        pl.when(last & (valid <= r * MOE_SUB))(functools.partial(zero_rows, r))


def moe_grouped(xs, te, tv, n_used, wg, wu, wd, tf):
    n_rows = xs.shape[0]
    E, _, F = wg.shape
    nf = F // tf

    def xrow(i, f, te, tv, nu):
        return (jnp.minimum(i, nu[0] - 1), 0)

    def fcol(i, f, nu):
        return jnp.where(i < nu[0], f, nf - 1)

    return pl.pallas_call(
        _moe_kernel,
        grid_spec=pltpu.PrefetchScalarGridSpec(
            num_scalar_prefetch=3,
            grid=(n_rows // MOE_TM, nf),
            in_specs=[pl.BlockSpec((MOE_TM, D_MODEL), xrow),
                      pl.BlockSpec((1, D_MODEL, tf),
                                   lambda i, f, te, tv, nu: (te[i], 0, fcol(i, f, nu))),
                      pl.BlockSpec((1, D_MODEL, tf),
                                   lambda i, f, te, tv, nu: (te[i], 0, fcol(i, f, nu))),
                      pl.BlockSpec((1, tf, D_MODEL),
                                   lambda i, f, te, tv, nu: (te[i], fcol(i, f, nu), 0))],
            out_specs=pl.BlockSpec((MOE_TM, D_MODEL), lambda i, f, te, tv, nu: (i, 0)),
            scratch_shapes=[pltpu.VMEM((MOE_TM, D_MODEL), F32)]),
        out_shape=jax.ShapeDtypeStruct((n_rows, D_MODEL), F32),
        compiler_params=_params("arbitrary", "arbitrary"),
        name="moe_grouped",
    )(te, tv, n_used, xs, wg, wu, wd)


def _combine_kernel(n_first, y0_ref, y1_ref, rt_ref, x_ref, mod_ref, gn_ref, xa_ref, xb_ref):
    f = rt_ref[:, 2:3] * y0_ref[...] + rt_ref[:, 3:4] * y1_ref[...]
    x2 = x_ref[...] + mod_ref[0, 5:6, :] * _rms(f, gn_ref[3:4, :])
    first = pl.program_id(0) < n_first

    @pl.when(first)
    def _():
        xa_ref[...] = x2

    @pl.when(jnp.logical_not(first))
    def _():
        xb_ref[...] = x2


def moe_combine(yg, route, x1, modt, gn, n_first):
    n_tok = x1.shape[0]
    n = n_tok // ROW_TILE
    row = lambda i: (i, 0)
    return pl.pallas_call(
        functools.partial(_combine_kernel, n_first),
        grid=(n,),
        in_specs=[pl.BlockSpec((ROW_TILE, D_MODEL), row),
                  pl.BlockSpec((ROW_TILE, D_MODEL), lambda i: (n + i, 0)),
                  pl.BlockSpec((ROW_TILE, LANES), row),
                  pl.BlockSpec((ROW_TILE, D_MODEL), row),
                  pl.BlockSpec((1, 6, D_MODEL), lambda i: (i, 0, 0)),
                  pl.BlockSpec((4, D_MODEL), lambda i: (0, 0))],
        out_specs=_two_part_specs(n_first, 0, 0),
        out_shape=(jax.ShapeDtypeStruct((n_first * ROW_TILE, D_MODEL), F32),
                   jax.ShapeDtypeStruct(((n - n_first) * ROW_TILE, D_MODEL), F32)),
        compiler_params=_params("arbitrary"),
        name="moe_combine",
    )(yg, yg, route, x1, modt, gn)


def _pack_w_in(w):
    d = w.shape[0]
    qw = N_Q_HEADS * HEAD_DIM
    kw = N_KV_HEADS * HEAD_DIM
    dw = DN_HEADS * DN_D
    o_k = qw
    o_v = o_k + kw
    o_qkv = o_v + kw
    o_z = o_qkv + 3 * dw
    o_ba = o_z + dw
    o_ga = o_ba + 4 * DN_HEADS
    o_gb = o_ga + D_MODEL
    main = jnp.concatenate([w[:, :qw], w[:, o_ga:o_gb], w[:, o_gb:], w[:, o_z:o_ba], w[:, o_qkv:o_z]],
                           axis=1)
    pad = jnp.zeros((d, N_SIDE - 2 * kw - (o_ga - o_ba)), w.dtype)
    side = jnp.concatenate([w[:, o_k:o_v], w[:, o_v:o_qkv], w[:, o_ba:o_ga], pad], axis=1)
    return main.astype(BF16), side.astype(BF16)


def _lane_consts(p):
    out = jnp.zeros((DN_HEADS, 1, LANES), F32)
    out = out.at[:, 0, 2].set(p[0])
    return out.at[:, 0, 3].set(p[1])


def kernel(x_prompt, x_sample, c, cache_k_l0, cache_v_l0, state_delta_l0, cache_k_l1, cache_v_l1,
           state_delta_l1, c_ctx, w_ada, b_ada, norm_g, w_in, attn_sink, conv_w, dn_a_log, dn_dt_bias,
           dn_norm_g, w_pa, w_pb, w_o, w_ff_gate, w_ff_up, w_ff_down, w_router, w_e_gate, w_e_up,
           w_e_down):
    Bc, Tc, _ = x_prompt.shape
    Bl, Tl, _ = x_sample.shape
    n_ctx = Bc * Tc
    n_tok = n_ctx + Bl * Tl
    depth = w_in.shape[0]
    caches = ((cache_k_l0, cache_v_l0, state_delta_l0), (cache_k_l1, cache_v_l1, state_delta_l1))

    x_parts = (x_prompt.reshape(n_ctx, D_MODEL), x_sample.reshape(Bl * Tl, D_MODEL), 0)
    cond =jnp.zeros((SUBLANES, D_MODEL), F32).at[:Bl].set(c).at[Bl].set(c_ctx)
    mod = ada_mod(cond, w_ada, b_ada)
    tile_row = np.concatenate([np.full(n_ctx // ROW_TILE, Bl),
                               np.repeat(np.arange(Bl), Tl // ROW_TILE)]).astype(np.int32)
    modt = mod[:, tile_row].reshape(depth, n_tok // ROW_TILE, 6, D_MODEL)
    cos, sin = rope_tables(Tl)

    outs = []
    h = norm_mod(x_parts[0], x_parts[1], norm_g[0, 0:1], modt[0])
    for l in range(depth):
        w_main, w_side = _pack_w_in(w_in[l])
        proj = matmul(h, w_main, BF16)
        side, k_out, v_out = side_proj(h, w_side, Bc, Tc)
        k_c, v_c, s_c = caches[l]
        oa_c = ctx_attention(proj, side, attn_sink[l], Bc, Tc)
        oa_l = lat_attention(proj, side, attn_sink[l], k_c, v_c, cos, sin, n_ctx, Bl, Tl)
        alog = _lane_consts(dn_a_log[l])
        dtb = _lane_consts(dn_dt_bias[l])
        gdn = dn_norm_g[l].reshape(1, DN_D)
        hps_c = min(DN_HEADS, DN_CHAINS // (2 * (Tc // DN_CHUNK)))
        hps_l = min(DN_HEADS, DN_CHAINS // (2 * (Tl // DN_CHUNK)))
        ob_c, s_fin = deltanet(proj, side, conv_w[l], alog, dtb, gdn, None, 0, Bc, Tc, hps_c)
        ob_l = deltanet(proj, side, conv_w[l], alog, dtb, gdn, s_c, n_ctx, Bl, Tl, hps_l)
        moe = l % 2 == 1
        i = l // 2
        wr = None
        if moe:
            wr = jnp.zeros((D_MODEL, LANES), F32).at[:, :N_EXPERTS].set(w_router[i])
        res = mix(oa_c, oa_l, ob_c, ob_l, proj, *x_parts, modt[l], norm_g[l], w_pa[l].astype(BF16),
                  w_pb[l].astype(BF16), w_o[l].astype(BF16), wr)
        if moe:
            x1, h2, route = res
            pos, n_rows, te, tv, n_used = route_plan(route, MOE_TM)
            xs = dispatch_rows(h2, pos, n_rows)
            ys = moe_grouped(xs, te, tv, n_used, w_e_gate[i], w_e_up[i], w_e_down[i], MOE_TF)
            yg = gather_rows(ys, pos)
            x_c, x_l = moe_combine(yg, route, x1, modt[l], norm_g[l], n_ctx // ROW_TILE)
            x_parts = (x_c, x_l, 0)
            h = None if l + 1 == depth else norm_mod(x_c, x_l, norm_g[l + 1, 0:1], modt[l + 1])
        else:
            x1, h2 = res
            nxt = l + 1 < depth
            r = ffn_dense(h2, w_ff_gate[i].astype(BF16), w_ff_up[i].astype(BF16),
                          w_ff_down[i].astype(BF16), x1, modt[l], norm_g[l],
                          modt[l + 1] if nxt else None, norm_g[l + 1] if nxt else None, FFN_TF)
            h = r[1] if nxt else None
            x_parts = (r[0], r[0], n_ctx // ROW_TILE)
        outs.append((k_out, v_out, s_fin))

    x_c, x_l, l_tile0 = x_parts
    y_prompt = x_c[:n_ctx].reshape(Bc, Tc, D_MODEL)
    y_sample = x_l[l_tile0 * ROW_TILE:].reshape(Bl, Tl, D_MODEL)
    return (y_prompt, y_sample, outs[0][0], outs[0][1], outs[0][2], outs[1][0], outs[1][1], outs[1][2])
```
